```python
import math
import jax, jax.numpy as jnp
from jax import lax
import numpy as np

D_MODEL = 1024
BATCH = 16
SEQ = 4096
DEPTH = 4
DEC_BATCH = 2
DEC_SEQ = 16384
PAST_LEN = 128

H_A = 6
HEAD_A = 64
A_W = H_A * HEAD_A
LORA_W = 64
LORA_A = 64
DECAY_SCALE = math.exp(-0.5)
H_B = 6
D_NOPE = 64
D_ROPE = 32
D_QK = D_NOPE + D_ROPE
D_V = 64
B_W = H_B * D_V
Q_RANK = 192
KV_RANK = 128
ROPE_THETA = 10000.0
Q_BLOCK = 128
C_GROUPS = 4
C_W = 256
C_GW = C_W // C_GROUPS
CHUNK = 128
D_MIX = A_W + B_W + C_W
CONV_W = 3
A_CONV_COLS = 3 * A_W + 2 * LORA_W + 2 * LORA_A
IN_COLS = A_CONV_COLS + A_W + Q_RANK + KV_RANK + D_ROPE + B_W + 3 * C_W
NORM_EPS = 1e-6
GN_EPS = 64e-5

kernel_name = "hybrid_rwkv7_mla_gmlp_encoder"


def _split(t, sizes):
    offs, acc = [], 0
    for s in sizes[:-1]:
        acc += s
        offs.append(acc)
    return jnp.split(t, offs, axis=-1)


def rms_norm(x, w, eps=NORM_EPS):
    xf = x.astype(jnp.float32)
    y = xf * lax.rsqrt(jnp.mean(xf * xf, axis=-1, keepdims=True) + eps)
    return (y * w.astype(jnp.float32)).astype(x.dtype)


def layer_norm(x, w, b, eps=NORM_EPS):
    xf = x.astype(jnp.float32)
    mu = jnp.mean(xf, axis=-1, keepdims=True)
    var = jnp.mean(jnp.square(xf - mu), axis=-1, keepdims=True)
    y = (xf - mu) * lax.rsqrt(var + eps)
    return (y * w.astype(jnp.float32) + b.astype(jnp.float32)).astype(x.dtype)


def l2_normalize(x, eps=1e-12):
    xf = x.astype(jnp.float32)
    n = jnp.sqrt(jnp.sum(xf * xf, axis=-1, keepdims=True))
    return (xf / jnp.maximum(n, eps)).astype(x.dtype)


def centred_conv3(x, wc):
    xp = jnp.pad(x, ((0, 0), (1, 1), (0, 0)))
    return xp[:, :-2] * wc[0] + xp[:, 1:-1] * wc[1] + xp[:, 2:] * wc[2]


def apply_rope_tail(x):
    S = x.shape[1]
    half = D_ROPE // 2
    freqs = ROPE_THETA ** (-jnp.arange(half, dtype=jnp.float32) / half)
    ang = jnp.arange(S, dtype=jnp.float32)[:, None] * freqs[None, :]
    cos = jnp.cos(ang)[None, :, None, :].astype(x.dtype)
    sin = jnp.sin(ang)[None, :, None, :].astype(x.dtype)
    x_n, x1, x2 = x[..., :D_NOPE], x[..., D_NOPE:D_NOPE + half], x[..., D_NOPE + half:]
    return jnp.concatenate([x_n, x1 * cos - x2 * sin, x2 * cos + x1 * sin], axis=-1)


def wkv7_scan(r, w, kk, akk, k_rep, v, reverse):
    dt = r.dtype
    xs = tuple(jnp.swapaxes(t.astype(jnp.float32), 0, 1) for t in (r, w, kk, akk, k_rep, v))
    Bn = r.shape[0]

    def step(S, inp):
        r_t, w_t, kk_t, akk_t, k_t, v_t = inp
        S = (S * w_t[:, :, None, :]
             - jnp.einsum('bhvk,bhk->bhv', S, kk_t)[..., None] * akk_t[:, :, None, :]
             + v_t[..., None] * k_t[:, :, None, :])
        return S, jnp.einsum('bhvk,bhk->bhv', S, r_t)

    S0 = jnp.zeros((Bn, H_A, HEAD_A, HEAD_A), jnp.float32)
    _, o = lax.scan(step, S0, xs, reverse=reverse)
    return jnp.swapaxes(o, 0, 1).astype(dt)


def rwkv7_branch(r, k, v, lw_f, lw_b, la_f, la_b, g, decay_w0, decay_up, iclr_a0, iclr_up,
                 k_k, k_a, r_k, gn_w, gn_b):
    Bn, S, _ = r.shape

    def heads(t):
        return t.reshape(Bn, S, H_A, HEAD_A)

    kk = l2_normalize(heads(k * k_k))

    def direction(d, lw, la, reverse):
        w = jnp.exp(-DECAY_SCALE * jax.nn.sigmoid(decay_w0[d] + jnp.tanh(lw) @ decay_up[d]))
        a = jax.nn.sigmoid(iclr_a0[d] + la @ iclr_up[d])
        k_rep = k * (1 + (a - 1) * k_a)
        return wkv7_scan(heads(r), heads(w), kk, heads(a) * kk, heads(k_rep), heads(v), reverse)

    o = direction(0, lw_f, la_f, False) + direction(1, lw_b, la_b, True)
    of = o.astype(jnp.float32)
    mu = jnp.mean(of, axis=-1, keepdims=True)
    var = jnp.mean(jnp.square(of - mu), axis=-1, keepdims=True)
    on = ((of - mu) * lax.rsqrt(var + GN_EPS)).astype(o.dtype)
    on = on * gn_w.reshape(H_A, HEAD_A) + gn_b.reshape(H_A, HEAD_A)
    bonus = jnp.sum(heads(r) * heads(k) * r_k, axis=-1, keepdims=True) * heads(v)
    return (on + bonus).reshape(Bn, S, A_W) * jax.nn.silu(g)


def bidir_attention(q, k, v):
    Bn, S, H, Dq = q.shape
    nb = S // Q_BLOCK
    qb = jnp.moveaxis(q.reshape(Bn, nb, Q_BLOCK, H, Dq), 1, 0)
    kf = k.astype(jnp.float32)
    scale = D_QK ** -0.5

    def one_block(q_blk):
        s = jnp.einsum('bqhd,bkhd->bhqk', q_blk.astype(jnp.float32), kf) * scale
        p = jax.nn.softmax(s, axis=-1).astype(v.dtype)
        return jnp.einsum('bhqk,bkhd->bqhd', p, v)

    o = lax.map(one_block, qb)
    return jnp.moveaxis(o, 0, 1).reshape(Bn, S, H, D_V)


def mla_branch(c_q, c_kv, k_rope, g, q_norm_w, w_uq, kv_norm_w, w_ukv, qk_q_norm_w, qk_k_norm_w):
    Bn, S, _ = c_q.shape
    q = (rms_norm(c_q, q_norm_w) @ w_uq).reshape(Bn, S, H_B, D_QK)
    kv = (rms_norm(c_kv, kv_norm_w) @ w_ukv).reshape(Bn, S, H_B, D_NOPE + D_V)
    k_nope, v = kv[..., :D_NOPE], kv[..., D_NOPE:]
    k = jnp.concatenate([k_nope, jnp.broadcast_to(k_rope[:, :, None, :], (Bn, S, H_B, D_ROPE))], axis=-1)
    q = apply_rope_tail(rms_norm(q, qk_q_norm_w))
    k = apply_rope_tail(rms_norm(k, qk_k_norm_w))
    o = bidir_attention(q, k, v)
    return o.reshape(Bn, S, B_W) * jax.nn.silu(g)


def sgu_branch(u, v, g, sgu_ln_w, sgu_ln_b, w_s, b_s):
    Bn, S, _ = u.shape
    nc = S // CHUNK
    vn = layer_norm(v, sgu_ln_w, sgu_ln_b).reshape(Bn, nc, CHUNK, C_GROUPS, C_GW)
    sv = jnp.einsum('gpq,bnqgc->bnpgc', w_s, vn) + b_s.T[None, None, :, :, None]
    return u * sv.reshape(Bn, S, C_W) * jax.nn.silu(g)


def _layer(x, c, p):
    mod = jax.nn.silu(c) @ p['ada_w'] + p['ada_b']
    shift, scale, gate = jnp.split(mod, 3, axis=-1)
    h = rms_norm(x, p['norm_w']) * (1 + scale[:, None, :]) + shift[:, None, :]
    z = h @ p['w_in']
    z_a = centred_conv3(z[..., :A_CONV_COLS], p['conv_a'])
    r, k, v, lw_f, lw_b, la_f, la_b = _split(z_a, [A_W, A_W, A_W, LORA_W, LORA_W, LORA_A, LORA_A])
    g_a, c_q, c_kv, k_rope, g_b, u_c, v_c, g_c = _split(
        z[..., A_CONV_COLS:], [A_W, Q_RANK, KV_RANK, D_ROPE, B_W, C_W, C_W, C_W])
    y_a = rwkv7_branch(r, k, v, lw_f, lw_b, la_f, la_b, g_a, p['decay_w0'], p['decay_up'],
                       p['iclr_a0'], p['iclr_up'], p['k_k'], p['k_a'], p['r_k'], p['gn_w'], p['gn_b'])
    y_b = mla_branch(c_q, c_kv, k_rope, g_b, p['q_norm_w'], p['w_uq'], p['kv_norm_w'], p['w_ukv'],
                     p['qk_q_norm_w'], p['qk_k_norm_w'])
    y_c = sgu_branch(u_c, v_c, g_c, p['sgu_ln_w'], p['sgu_ln_b'], p['w_s'], p['b_s'])
    y = jnp.concatenate([y_a, y_b, y_c], axis=-1) @ p['w_out']
    return x + gate[:, None, :] * y


def setup_inputs(seed: int = 0) -> dict:
    key = jax.random.key(seed)
    ks = jax.random.split(key, 32)
    n = lambda i, shape: jax.random.normal(ks[i], shape, jnp.float32)
    D = D_MODEL
    conv_base = jnp.array([0.25, 1.0, 0.25], jnp.float32)[None, :, None]
    return {
        'x_prompt': n(0, (BATCH, SEQ, D)),
        'x_sample': n(1, (DEC_BATCH, DEC_SEQ, D)),
        'c_prompt': n(2, (BATCH, D)),
        'c_sample': n(3, (DEC_BATCH, D)),
        'norm_w': 1.0 + 0.02 * n(4, (DEPTH, D)),
        'ada_w': n(5, (DEPTH, D, 3 * D)) * (0.5 * D ** -0.5),
        'ada_b': 0.1 * n(6, (DEPTH, 3 * D)),
        'w_in': n(7, (DEPTH, D, IN_COLS)) * D ** -0.5,
        'conv_a': conv_base + 0.1 * n(8, (DEPTH, CONV_W, A_CONV_COLS)),
        'decay_w0': 0.5 * n(9, (DEPTH, 2, A_W)),
        'decay_up': n(10, (DEPTH, 2, LORA_W, A_W)) * (0.5 * LORA_W ** -0.5),
        'iclr_a0': 0.5 * n(11, (DEPTH, 2, A_W)),
        'iclr_up': n(12, (DEPTH, 2, LORA_A, A_W)) * (0.5 * LORA_A ** -0.5),
        'k_k': 0.85 + 0.05 * n(13, (DEPTH, A_W)),
        'k_a': 1.0 + 0.05 * n(14, (DEPTH, A_W)),
        'r_k': 0.1 * n(15, (DEPTH, H_A, HEAD_A)),
        'gn_w': 1.0 + 0.02 * n(16, (DEPTH, A_W)),
        'gn_b': 0.02 * n(17, (DEPTH, A_W)),
        'q_norm_w': 1.0 + 0.02 * n(18, (DEPTH, Q_RANK)),
        'w_uq': n(19, (DEPTH, Q_RANK, H_B * D_QK)) * Q_RANK ** -0.5,
        'kv_norm_w': 1.0 + 0.02 * n(20, (DEPTH, KV_RANK)),
        'w_ukv': n(21, (DEPTH, KV_RANK, H_B * (D_NOPE + D_V))) * KV_RANK ** -0.5,
        'qk_q_norm_w': 1.0 + 0.02 * n(22, (DEPTH, D_QK)),
        'qk_k_norm_w': 1.0 + 0.02 * n(23, (DEPTH, D_QK)),
        'sgu_ln_w': 1.0 + 0.02 * n(24, (DEPTH, C_W)),
        'sgu_ln_b': 0.02 * n(25, (DEPTH, C_W)),
        'w_s': n(26, (DEPTH, C_GROUPS, CHUNK, CHUNK)) * (0.5 * CHUNK ** -0.5),
        'b_s': 1.0 + 0.1 * n(27, (DEPTH, C_GROUPS, CHUNK)),
        'w_out': n(28, (DEPTH, D_MIX, D)) * D_MIX ** -0.5,
    }


def reference(x_prompt, x_sample, c_prompt, c_sample, norm_w, ada_w, ada_b, w_in, conv_a,
              decay_w0, decay_up, iclr_a0, iclr_up, k_k, k_a, r_k, gn_w, gn_b,
              q_norm_w, w_uq, kv_norm_w, w_ukv, qk_q_norm_w, qk_k_norm_w,
              sgu_ln_w, sgu_ln_b, w_s, b_s, w_out):
    y_prompt, y_sample = x_prompt, x_sample
    for l in range(DEPTH):
        p = dict(norm_w=norm_w[l], ada_w=ada_w[l], ada_b=ada_b[l], w_in=w_in[l], conv_a=conv_a[l],
                 decay_w0=decay_w0[l], decay_up=decay_up[l], iclr_a0=iclr_a0[l], iclr_up=iclr_up[l],
                 k_k=k_k[l], k_a=k_a[l], r_k=r_k[l], gn_w=gn_w[l], gn_b=gn_b[l],
                 q_norm_w=q_norm_w[l], w_uq=w_uq[l], kv_norm_w=kv_norm_w[l], w_ukv=w_ukv[l],
                 qk_q_norm_w=qk_q_norm_w[l], qk_k_norm_w=qk_k_norm_w[l],
                 sgu_ln_w=sgu_ln_w[l], sgu_ln_b=sgu_ln_b[l], w_s=w_s[l], b_s=b_s[l], w_out=w_out[l])
        y_prompt = _layer(y_prompt, c_prompt, p)
        y_sample = _layer(y_sample, c_sample, p)
    return (y_prompt, y_sample)
```

```python
import functools
import math

import jax
import jax.numpy as jnp
from jax import lax
from jax.experimental import pallas as pl
from jax.experimental.pallas import tpu as pltpu

F32 = jnp.float32
BF16 = jnp.bfloat16

LANES = 128
SUBLANES = 8

D_MODEL = 1024
H_A = 6
HEAD_A = 64
A_W = H_A * HEAD_A
LORA_W = 64
LORA_A = 64
DECAY_SCALE = math.exp(-0.5)
H_B = 6
D_NOPE = 64
D_ROPE = 32
D_QK = D_NOPE + D_ROPE
D_V = 64
B_W = H_B * D_V
Q_RANK = 192
KV_RANK = 128
ROPE_THETA = 10000.0
C_GROUPS = 4
C_W = 256
C_GW = C_W // C_GROUPS
CHUNK = 128
A_CONV_COLS = 3 * A_W + 2 * LORA_W + 2 * LORA_A
NORM_EPS = 1e-6
GN_EPS = 64e-5

QK_PAD = LANES
QKW = H_B * QK_PAD
CM_W = 512
E_W = 3 * C_W
VMEM_LIMIT = 56 * 1024 * 1024


def _cparams(sem):
    return pltpu.CompilerParams(dimension_semantics=sem, vmem_limit_bytes=VMEM_LIMIT)


def _silu(x):
    return x * jax.nn.sigmoid(x)


def _dot(a, b):
    return jnp.dot(a, b, preferred_element_type=F32)


def _seg_sum(x, ones_bf16):
    hi = x.astype(BF16)
    lo = (x - hi.astype(F32)).astype(BF16)
    return _dot(hi, ones_bf16) + _dot(lo, ones_bf16)


def _block_ones(width, block):
    i = jnp.arange(width) // block
    return (i[:, None] == i[None, :]).astype(BF16)


def _mod_kernel(c_ref, w_ref, b_ref, o_ref):
    sc = _silu(c_ref[...])
    o_ref[0] = _dot(sc.astype(BF16), w_ref[0].astype(BF16)) + b_ref[0]


def _ada_mod(c_all, ada_w, ada_b):
    n_layers, d, d3 = ada_w.shape
    nb = c_all.shape[0]
    tn = 768
    return pl.pallas_call(
        _mod_kernel,
        grid=(n_layers, d3 // tn),
        in_specs=[
            pl.BlockSpec((nb, d), lambda l, j: (0, 0)),
            pl.BlockSpec((1, d, tn), lambda l, j: (l, 0, j)),
            pl.BlockSpec((1, 1, tn), lambda l, j: (l, 0, j)),
        ],
        out_specs=pl.BlockSpec((1, nb, tn), lambda l, j: (l, 0, j)),
        out_shape=jax.ShapeDtypeStruct((n_layers, nb, d3), F32),
        compiler_params=_cparams(("parallel", "parallel")),
        name="ada_mod",
    )(c_all, ada_w, ada_b.reshape(n_layers, 1, d3))


IN_GROUPS = (A_CONV_COLS, A_W, CM_W, B_W, E_W)


def _inproj_kernel(x_ref, mod_ref, nw_ref, w_ref, za_ref, ga_ref, cm_ref, gb_ref, e_ref):
    x = x_ref[0]
    ms = jnp.mean(x * x, axis=-1, keepdims=True)
    y = x * lax.rsqrt(ms + NORM_EPS) * nw_ref[...]
    h = (y * (1.0 + mod_ref[0, 1:2, :]) + mod_ref[0, 0:1, :]).astype(BF16)
    off = 0
    for ref, width in zip((za_ref, ga_ref, cm_ref, gb_ref, e_ref), IN_GROUPS):
        ref[0] = _dot(h, w_ref[:, off:off + width])
        off += width


def _in_proj(x, mod3, norm_w, w_pad, tm):
    b, s, d = x.shape
    wtot = w_pad.shape[1]
    tok = lambda width: pl.BlockSpec((1, tm, width), lambda bi, i: (bi, i, 0))
    return pl.pallas_call(
        _inproj_kernel,
        grid=(b, s // tm),
        in_specs=[
            tok(d),
            pl.BlockSpec((1, 3, d), lambda bi, i: (bi, 0, 0)),
            pl.BlockSpec((1, d), lambda bi, i: (0, 0)),
            pl.BlockSpec((d, wtot), lambda bi, i: (0, 0)),
        ],
        out_specs=[tok(wd) for wd in IN_GROUPS],
        out_shape=[jax.ShapeDtypeStruct((b, s, wd), F32) for wd in IN_GROUPS],
        compiler_params=_cparams(("parallel", "parallel")),
        name="in_proj",
    )(x, mod3, norm_w.reshape(1, d), w_pad)


def _rwkv_prep_kernel(z_ref, zp_ref, zn_ref, conv_ref, dw0_ref, dup_ref, ia0_ref, iup_ref,
                      kkw_ref, ka_ref, rk_ref, ones_ref,
                      r_ref, kk_ref, v_ref, bon_ref, w_ref, akk_ref, krep_ref, *, ts):
    i = pl.program_id(1)
    last = pl.num_programs(1) - 1
    z = z_ref[0]
    prev_row = jnp.where(i > 0, zp_ref[0, SUBLANES - 1:SUBLANES, :], 0.0)
    next_row = jnp.where(i < last, zn_ref[0, 0:1, :], 0.0)
    row = lax.broadcasted_iota(jnp.int32, z.shape, 0)
    z_prev = jnp.where(row == 0, prev_row, pltpu.roll(z, 1, 0))
    z_next = jnp.where(row == ts - 1, next_row, pltpu.roll(z, ts - 1, 0))
    za = z_prev * conv_ref[0:1, :] + z * conv_ref[1:2, :] + z_next * conv_ref[2:3, :]

    r = za[:, 0:A_W]
    k = za[:, A_W:2 * A_W]
    v = za[:, 2 * A_W:3 * A_W]
    low = za[:, 3 * A_W:3 * A_W + 2 * LANES]
    ones = ones_ref[...]

    kx = k * kkw_ref[...]
    nrm = jnp.sqrt(_seg_sum(kx * kx, ones))
    kk = kx / jnp.maximum(nrm, 1e-12)
    bonus = _seg_sum(r * k * rk_ref[...], ones) * v

    r_ref[0] = r
    kk_ref[0] = kk
    v_ref[0] = v
    bon_ref[0] = bonus

    tl = jnp.tanh(low[:, 0:LANES]).astype(BF16)
    la = low[:, LANES:2 * LANES].astype(BF16)
    lane = lax.broadcasted_iota(jnp.int32, tl.shape, 1)
    for d in range(2):
        sel = (lane // LORA_W) == d
        dec = _dot(jnp.where(sel, tl, 0), dup_ref[d]) + dw0_ref[d:d + 1, :]
        w = jnp.exp(-DECAY_SCALE * jax.nn.sigmoid(dec))
        a = jax.nn.sigmoid(_dot(jnp.where(sel, la, 0), iup_ref[d]) + ia0_ref[d:d + 1, :])
        w_ref[d, 0] = w
        akk_ref[d, 0] = a * kk
        krep_ref[d, 0] = k * (1.0 + (a - 1.0) * ka_ref[...])


def _rwkv_prep(za, conv_a, decay_w0, decay_up2, iclr_a0, iclr_up2, k_k, k_a, r_k, ones64, ts):
    b, s, wz = za.shape
    nsub = ts // SUBLANES
    tok = pl.BlockSpec((1, ts, A_W), lambda bi, i: (bi, i, 0))
    tok2 = pl.BlockSpec((2, 1, ts, A_W), lambda bi, i: (0, bi, i, 0))
    full = lambda shape: pl.BlockSpec(shape, lambda bi, i: (0,) * len(shape))
    o1 = jax.ShapeDtypeStruct((b, s, A_W), F32)
    o2 = jax.ShapeDtypeStruct((2, b, s, A_W), F32)
    return pl.pallas_call(
        functools.partial(_rwkv_prep_kernel, ts=ts),
        grid=(b, s // ts),
        in_specs=[
            pl.BlockSpec((1, ts, wz), lambda bi, i: (bi, i, 0)),
            pl.BlockSpec((1, SUBLANES, wz), lambda bi, i: (bi, jnp.maximum(i * nsub - 1, 0), 0)),
            pl.BlockSpec((1, SUBLANES, wz),
                         lambda bi, i: (bi, jnp.minimum((i + 1) * nsub, s // SUBLANES - 1), 0)),
            full((3, wz)), full((2, A_W)), full((2, LANES, A_W)), full((2, A_W)), full((2, LANES, A_W)),
            full((1, A_W)), full((1, A_W)), full((1, A_W)), full((A_W, A_W)),
        ],
        out_specs=[tok, tok, tok, tok, tok2, tok2, tok2],
        out_shape=[o1, o1, o1, o1, o2, o2, o2],
        compiler_params=_cparams(("parallel", "parallel")),
        name="rwkv_prep",
    )(za, za, za, conv_a, decay_w0, decay_up2, iclr_a0, iclr_up2,
      k_k.reshape(1, A_W), k_a.reshape(1, A_W), r_k.reshape(1, A_W), ones64)


K_UNROLL = SUBLANES


def _scan_kernel(r_ref, kk_ref, v_ref, w_ref, akk_ref, krep_ref, o_ref, st_ref, *, tsteps, vo_n):
    g = pl.program_id(0)

    @pl.when(pl.program_id(1) == 0)
    def _init():
        st_ref[...] = jnp.zeros_like(st_ref)

    n_acc = max(1, 4 // vo_n)
    vsl = [slice(vo * SUBLANES, (vo + 1) * SUBLANES) for vo in range(vo_n)]
    zero = jnp.zeros((SUBLANES, LANES), F32)

    def bcast(ref_row):
        return jnp.broadcast_to(ref_row, (SUBLANES, LANES))

    def combine(acc):
        out = []
        for vo in range(vo_n):
            tot = acc[vo * n_acc]
            for a in range(1, n_acc):
                tot = tot + acc[vo * n_acc + a]
            out.append(tot)
        return out

    def step(t, carry):
        tt = t + g * (tsteps - 1 - 2 * t)
        vs = [v_ref[tt, vsl[vo], :] for vo in range(vo_n)]

        def sa_body(kb, acc):
            acc = list(acc)
            for kj in range(K_UNROLL):
                k = kb * K_UNROLL + kj
                kkb = bcast(kk_ref[tt, pl.ds(k, 1), :])
                for vo in range(vo_n):
                    idx = vo * n_acc + (kj % n_acc)
                    acc[idx] = acc[idx] + st_ref[k, vsl[vo], :] * kkb
            return tuple(acc)

        sa = combine(lax.fori_loop(0, HEAD_A // K_UNROLL, sa_body, (zero,) * (vo_n * n_acc)))

        def up_body(kb, acc):
            acc = list(acc)
            for kj in range(K_UNROLL):
                k = kb * K_UNROLL + kj
                wb = bcast(w_ref[0, tt, pl.ds(k, 1), :])
                ab = bcast(akk_ref[0, tt, pl.ds(k, 1), :])
                kb_ = bcast(krep_ref[0, tt, pl.ds(k, 1), :])
                rb = bcast(r_ref[tt, pl.ds(k, 1), :])
                for vo in range(vo_n):
                    s_new = st_ref[k, vsl[vo], :] * wb - sa[vo] * ab + vs[vo] * kb_
                    st_ref[k, vsl[vo], :] = s_new
                    idx = vo * n_acc + (kj % n_acc)
                    acc[idx] = acc[idx] + s_new * rb
            return tuple(acc)

        o = combine(lax.fori_loop(0, HEAD_A // K_UNROLL, up_body, (zero,) * (vo_n * n_acc)))
        for vo in range(vo_n):
            o_ref[0, tt, vsl[vo], :] = o[vo]
        return carry

    lax.fori_loop(0, tsteps, step, 0)


def _rwkv_scan(r_s, kk_s, v_s, w_s, akk_s, krep_s, tsteps):
    s = r_s.shape[0]
    vrows = v_s.shape[1]
    vo_n = vrows // SUBLANES
    nb = s // tsteps
    tblk = lambda g, i: i + g * (nb - 1 - 2 * i)
    shared = lambda rows: pl.BlockSpec((tsteps, rows, LANES), lambda g, i: (tblk(g, i), 0, 0))
    perdir = lambda rows: pl.BlockSpec((1, tsteps, rows, LANES), lambda g, i: (g, tblk(g, i), 0, 0))
    return pl.pallas_call(
        functools.partial(_scan_kernel, tsteps=tsteps, vo_n=vo_n),
        grid=(2, nb),
        in_specs=[shared(HEAD_A), shared(HEAD_A), shared(vrows),
                  perdir(HEAD_A), perdir(HEAD_A), perdir(HEAD_A)],
        out_specs=perdir(vrows),
        out_shape=jax.ShapeDtypeStruct((2, s, vrows, LANES), F32),
        scratch_shapes=[pltpu.VMEM((HEAD_A, vrows, LANES), F32)],
        compiler_params=_cparams(("arbitrary", "arbitrary")),
        name="rwkv_scan",
    )(r_s, kk_s, v_s, w_s, akk_s, krep_s)


def _value_parts(batch):
    recs = batch * H_A
    parts = 1
    while parts < SUBLANES and recs * parts * 2 <= LANES:
        parts *= 2
    return parts


def _to_scan_k(x, parts):
    lead = x.shape[:-3]
    b, s, _ = x.shape[-3:]
    nl = len(lead)
    x = x.reshape(*lead, b, s, H_A, HEAD_A)
    x = jnp.transpose(x, tuple(range(nl)) + (nl + 1, nl + 3, nl, nl + 2))
    x = jnp.broadcast_to(x[..., None], (*lead, s, HEAD_A, b, H_A, parts))
    x = x.reshape(*lead, s, HEAD_A, b * H_A * parts)
    pad = [(0, 0)] * (nl + 2) + [(0, LANES - b * H_A * parts)]
    return jnp.pad(x, pad)


def _to_scan_v(x, parts):
    b, s, _ = x.shape
    rows = HEAD_A // parts
    x = x.reshape(b, s, H_A, rows, parts)
    x = jnp.transpose(x, (1, 3, 0, 2, 4)).reshape(s, rows, b * H_A * parts)
    return jnp.pad(x, ((0, 0), (0, 0), (0, LANES - b * H_A * parts)))


def _from_scan_v(o, batch, parts):
    _, s, rows, _ = o.shape
    o = o[..., :batch * H_A * parts].reshape(2, s, rows, batch, H_A, parts)
    return jnp.transpose(o, (0, 3, 1, 4, 2, 5)).reshape(2, batch, s, A_W)


def _rope(x, c_t, s1_t, s2_t):
    width = x.shape[-1]
    half = D_ROPE // 2
    return x * c_t + pltpu.roll(x, width - half, 1) * s1_t + pltpu.roll(x, half, 1) * s2_t


def _mla_prep_kernel(cm_ref, c_ref, s1_ref, s2_ref, qnw_ref, wuq_ref, kvnw_ref, wuk_ref, wuv_ref,
                     place_ref, qkq_ref, qkk_ref, ones_ref, q_ref, k_ref, v_ref):
    cm = cm_ref[0]
    ones = ones_ref[...]
    c_t = jnp.concatenate([c_ref[...]] * H_B, axis=1)
    s1_t = jnp.concatenate([s1_ref[...]] * H_B, axis=1)
    s2_t = jnp.concatenate([s2_ref[...]] * H_B, axis=1)

    cq = cm[:, 0:2 * LANES]
    msq = jnp.sum(cq * cq, axis=-1, keepdims=True) * (1.0 / Q_RANK)
    cqn = (cq * lax.rsqrt(msq + NORM_EPS) * qnw_ref[...]).astype(BF16)
    q = _dot(cqn, wuq_ref[...])
    q = q * lax.rsqrt(_seg_sum(q * q, ones) * (1.0 / D_QK) + NORM_EPS) * qkq_ref[...]
    q_ref[0] = _rope(q, c_t, s1_t, s2_t).astype(BF16)

    ckv = cm[:, 2 * LANES:3 * LANES]
    mskv = jnp.mean(ckv * ckv, axis=-1, keepdims=True)
    ckvn = (ckv * lax.rsqrt(mskv + NORM_EPS) * kvnw_ref[...]).astype(BF16)
    v_ref[0] = _dot(ckvn, wuv_ref[...]).astype(BF16)

    kr = cm[:, 3 * LANES:4 * LANES]
    place = place_ref[...]
    kr_hi = kr.astype(BF16)
    rem = kr - kr_hi.astype(F32)
    kr_mid = rem.astype(BF16)
    kr_lo = (rem - kr_mid.astype(F32)).astype(BF16)
    k = _dot(ckvn, wuk_ref[...]) + _dot(kr_hi, place) + _dot(kr_mid, place) + _dot(kr_lo, place)
    k = k * lax.rsqrt(_seg_sum(k * k, ones) * (1.0 / D_QK) + NORM_EPS) * qkk_ref[...]
    k_ref[0] = _rope(k, c_t, s1_t, s2_t).astype(BF16)


def _mla_prep(cm, tabs, mw, ts):
    b, s, _ = cm.shape
    full = lambda shape: pl.BlockSpec(shape, lambda bi, i: (0,) * len(shape))
    tab = pl.BlockSpec((ts, LANES), lambda bi, i: (i, 0))
    out = pl.BlockSpec((1, ts, QKW), lambda bi, i: (bi, i, 0))
    osh = jax.ShapeDtypeStruct((b, s, QKW), BF16)
    return pl.pallas_call(
        _mla_prep_kernel,
        grid=(b, s // ts),
        in_specs=[
            pl.BlockSpec((1, ts, CM_W), lambda bi, i: (bi, i, 0)), tab, tab, tab,
            full((1, 2 * LANES)), full((2 * LANES, QKW)), full((1, LANES)), full((LANES, QKW)),
            full((LANES, QKW)), full((LANES, QKW)), full((1, QKW)), full((1, QKW)), full((QKW, QKW)),
        ],
        out_specs=[out, out, out],
        out_shape=[osh, osh, osh],
        compiler_params=_cparams(("parallel", "parallel")),
        name="mla_prep",
    )(cm, *tabs, mw["qnw"], mw["wuq"], mw["kvnw"], mw["wuk"], mw["wuv"], mw["place"],
      mw["qkq"], mw["qkk"], mw["ones128"])


def _rope_tables(s):
    half = D_ROPE // 2
    freqs = ROPE_THETA ** (-jnp.arange(half, dtype=F32) / half)
    ang = jnp.arange(s, dtype=F32)[:, None] * freqs[None, :]
    cos, sin = jnp.cos(ang), jnp.sin(ang)
    z = lambda n: jnp.zeros((s, n), F32)
    c_t = jnp.concatenate([jnp.ones((s, D_NOPE), F32), cos, cos, z(QK_PAD - D_QK)], axis=1)
    s1_t = jnp.concatenate([z(D_NOPE), -sin, z(half), z(QK_PAD - D_QK)], axis=1)
    s2_t = jnp.concatenate([z(D_NOPE), z(half), sin, z(QK_PAD - D_QK)], axis=1)
    return c_t, s1_t, s2_t


def _attn_kernel(q_ref, k_ref, v_ref, o_ref, *, tk):
    tq = q_ref.shape[1]
    nk = k_ref.shape[1] // tk
    out = None
    for h in range(2):
        hs = slice(h * QK_PAD, (h + 1) * QK_PAD)
        q = q_ref[0, :, hs]

        def body(j, carry, hs=hs, q=q):
            m, l, acc = carry
            rows = pl.ds(pl.multiple_of(j * tk, tk), tk)
            s = lax.dot_general(q, k_ref[0, rows, hs], (((1,), (1,)), ((), ())),
                                preferred_element_type=F32)
            m_new = jnp.maximum(m, jnp.max(s, axis=-1, keepdims=True))
            alpha = jnp.exp(m - m_new)
            p = jnp.exp(s - m_new)
            l = alpha * l + jnp.sum(p, axis=-1, keepdims=True)
            acc = alpha * acc + _dot(p.astype(BF16), v_ref[0, rows, hs])
            return m_new, l, acc

        m0 = jnp.full((tq, 1), -1e30, F32)
        l0 = jnp.zeros((tq, 1), F32)
        a0 = jnp.zeros((tq, QK_PAD), F32)
        _, l, acc = lax.fori_loop(0, nk, body, (m0, l0, a0))
        part = acc / l
        out = part if out is None else out + part
    o_ref[0] = out


def _attention(q, k, v, tq, tk):
    b, s, _ = q.shape
    pair_w = 2 * QK_PAD
    return pl.pallas_call(
        functools.partial(_attn_kernel, tk=tk),
        grid=(b, H_B // 2, s // tq),
        in_specs=[
            pl.BlockSpec((1, tq, pair_w), lambda bi, hp, i: (bi, i, hp)),
            pl.BlockSpec((1, s, pair_w), lambda bi, hp, i: (bi, 0, hp)),
            pl.BlockSpec((1, s, pair_w), lambda bi, hp, i: (bi, 0, hp)),
        ],
        out_specs=pl.BlockSpec((1, tq, 2 * D_V), lambda bi, hp, i: (bi, i, hp)),
        out_shape=jax.ShapeDtypeStruct((b, s, B_W), F32),
        compiler_params=_cparams(("parallel", "parallel", "parallel")),
        name="mla_attention",
    )(q, k, v)


def _sgu_kernel(e_ref, lnw_ref, lnb_ref, wcat_ref, bias_ref, y_ref, *, ts):
    e = e_ref[0]
    u = e[:, 0:C_W]
    v = e[:, C_W:2 * C_W]
    g = e[:, 2 * C_W:3 * C_W]
    mu = jnp.mean(v, axis=-1, keepdims=True)
    d = v - mu
    var = jnp.mean(d * d, axis=-1, keepdims=True)
    vn = d * lax.rsqrt(var + NORM_EPS) * lnw_ref[...] + lnb_ref[...]
    group = lax.broadcasted_iota(jnp.int32, (CHUNK, C_W), 1) // C_GW
    wcat = wcat_ref[...]
    bias = bias_ref[...]
    for c in range(ts // CHUNK):
        rows = slice(c * CHUNK, (c + 1) * CHUNK)
        vc = vn[rows, :]
        stack = jnp.concatenate([jnp.where(group == gi, vc, 0.0) for gi in range(C_GROUPS)], axis=0)
        sv = _dot(wcat, stack.astype(BF16)) + bias
        y_ref[0, rows, :] = u[rows, :] * sv * _silu(g[rows, :])


def _sgu(e, lnw, lnb, wcat, bias, ts):
    b, s, _ = e.shape
    full = lambda shape: pl.BlockSpec(shape, lambda bi, i: (0,) * len(shape))
    return pl.pallas_call(
        functools.partial(_sgu_kernel, ts=ts),
        grid=(b, s // ts),
        in_specs=[pl.BlockSpec((1, ts, E_W), lambda bi, i: (bi, i, 0)),
                  full((1, C_W)), full((1, C_W)), full((CHUNK, C_GROUPS * CHUNK)), full((CHUNK, C_W))],
        out_specs=pl.BlockSpec((1, ts, C_W), lambda bi, i: (bi, i, 0)),
        out_shape=jax.ShapeDtypeStruct((b, s, C_W), F32),
        compiler_params=_cparams(("parallel", "parallel")),
        name="sgu",
    )(e, lnw, lnb, wcat, bias)


def _out_kernel(x_ref, mod_ref, o2_ref, bon_ref, ga_ref, oat_ref, gb_ref, yc_ref,
                gnw_ref, gnb_ref, ones_ref, wa_ref, wb_ref, wc_ref, out_ref):
    ones = ones_ref[...]
    o = o2_ref[0, 0] + o2_ref[1, 0]
    mu = _seg_sum(o, ones) * (1.0 / HEAD_A)
    d = o - mu
    var = _seg_sum(d * d, ones) * (1.0 / HEAD_A)
    on = d * lax.rsqrt(var + GN_EPS)
    ya = (on * gnw_ref[...] + gnb_ref[...] + bon_ref[0]) * _silu(ga_ref[0])
    yb = oat_ref[0] * _silu(gb_ref[0])
    y = (_dot(ya.astype(BF16), wa_ref[...]) + _dot(yb.astype(BF16), wb_ref[...])
         + _dot(yc_ref[0].astype(BF16), wc_ref[...]))
    out_ref[0] = x_ref[0] + mod_ref[0, 2:3, :] * y


def _out_proj(x, mod3, o2, bon, ga, oat, gb, yc, gnw, gnb, ones64, wa, wb, wc, tm):
    b, s, d = x.shape
    tok = lambda width: pl.BlockSpec((1, tm, width), lambda bi, i: (bi, i, 0))
    full = lambda shape: pl.BlockSpec(shape, lambda bi, i: (0,) * len(shape))
    return pl.pallas_call(
        _out_kernel,
        grid=(b, s // tm),
        in_specs=[
            tok(d), pl.BlockSpec((1, 3, d), lambda bi, i: (bi, 0, 0)),
            pl.BlockSpec((2, 1, tm, A_W), lambda bi, i: (0, bi, i, 0)),
            tok(A_W), tok(A_W), tok(B_W), tok(B_W), tok(C_W),
            full((1, A_W)), full((1, A_W)), full((A_W, A_W)),
            full((A_W, d)), full((B_W, d)), full((C_W, d)),
        ],
        out_specs=tok(d),
        out_shape=jax.ShapeDtypeStruct((b, s, d), F32),
        compiler_params=_cparams(("parallel", "parallel")),
        name="out_proj",
    )(x, mod3, o2, bon, ga, oat, gb, yc, gnw, gnb, ones64, wa, wb, wc)


def _pad_cols(w, width):
    return jnp.pad(w, ((0, 0), (0, width - w.shape[1])))


def _per_head_cols(w, real, padded):
    kdim = w.shape[0]
    w = w.reshape(kdim, -1, real)
    return jnp.pad(w, ((0, 0), (0, 0), (0, padded - real))).reshape(kdim, -1)


def _layer_params(l, norm_w, w_in, conv_a, decay_w0, decay_up, iclr_a0, iclr_up, k_k, k_a, r_k,
                  gn_w, gn_b, q_norm_w, w_uq, kv_norm_w, w_ukv, qk_q_norm_w, qk_k_norm_w,
                  sgu_ln_w, sgu_ln_b, w_s, b_s, w_out):
    wi = w_in[l]
    o_ga = A_CONV_COLS
    o_cq = o_ga + A_W
    o_ckv = o_cq + Q_RANK
    o_kr = o_ckv + KV_RANK
    o_gb = o_kr + D_ROPE
    o_e = o_gb + B_W
    w_pad = jnp.concatenate([
        wi[:, :o_cq],
        _pad_cols(wi[:, o_cq:o_ckv], 2 * LANES),
        wi[:, o_ckv:o_kr],
        _pad_cols(wi[:, o_kr:o_gb], LANES),
        wi[:, o_gb:],
    ], axis=1).astype(BF16)

    zeros64 = jnp.zeros((LORA_W, A_W), F32)
    dup2 = jnp.stack([jnp.concatenate([decay_up[l, 0], zeros64], 0),
                      jnp.concatenate([zeros64, decay_up[l, 1]], 0)]).astype(BF16)
    iup2 = jnp.stack([jnp.concatenate([iclr_up[l, 0], zeros64], 0),
                      jnp.concatenate([zeros64, iclr_up[l, 1]], 0)]).astype(BF16)

    scale = D_QK ** -0.5
    wukv = w_ukv[l].reshape(KV_RANK, H_B, D_NOPE + D_V)
    wuk = jnp.pad(wukv[:, :, :D_NOPE], ((0, 0), (0, 0), (0, QK_PAD - D_NOPE))).reshape(KV_RANK, QKW)
    wv = wukv[:, :, D_NOPE:].reshape(KV_RANK, H_B // 2, 2, D_V)
    zv = jnp.zeros((KV_RANK, H_B // 2, D_V), F32)
    wuv = jnp.stack([wv[:, :, 0], zv, zv, wv[:, :, 1]], axis=2).reshape(KV_RANK, QKW)
    lane = jnp.arange(QKW)
    place = ((lane[None, :] % QK_PAD) == (D_NOPE + jnp.arange(LANES)[:, None])) & (jnp.arange(LANES)[:, None] < D_ROPE)
    head_w = lambda w: jnp.tile(jnp.pad(w, (0, QK_PAD - D_QK)), H_B).reshape(1, QKW)
    mw = dict(
        qnw=jnp.pad(q_norm_w[l], (0, 2 * LANES - Q_RANK)).reshape(1, 2 * LANES),
        wuq=jnp.pad(_per_head_cols(w_uq[l], D_QK, QK_PAD), ((0, 2 * LANES - Q_RANK), (0, 0))).astype(BF16),
        kvnw=kv_norm_w[l].reshape(1, KV_RANK),
        wuk=wuk.astype(BF16),
        wuv=wuv.astype(BF16),
        place=place.astype(BF16),
        qkq=head_w(qk_q_norm_w[l]) * scale,
        qkk=head_w(qk_k_norm_w[l]),
        ones128=_block_ones(QKW, QK_PAD),
    )
    wcat = jnp.transpose(w_s[l], (1, 0, 2)).reshape(CHUNK, C_GROUPS * CHUNK).astype(BF16)
    bias = jnp.repeat(b_s[l].T, C_GW, axis=1)
    wo = w_out[l].astype(BF16)
    return dict(
        norm_w=norm_w[l], w_pad=w_pad, conv_a=conv_a[l], decay_w0=decay_w0[l], dup2=dup2,
        iclr_a0=iclr_a0[l], iup2=iup2, k_k=k_k[l], k_a=k_a[l], r_k=r_k[l].reshape(A_W),
        gnw=gn_w[l].reshape(1, A_W), gnb=gn_b[l].reshape(1, A_W), ones64=_block_ones(A_W, HEAD_A),
        mw=mw, lnw=sgu_ln_w[l].reshape(1, C_W), lnb=sgu_ln_b[l].reshape(1, C_W), wcat=wcat, bias=bias,
        wa=wo[:A_W], wb=wo[A_W:A_W + B_W], wc=wo[A_W + B_W:],
    )


TOKEN_TILE = 256
ATTN_TQ = 256
ATTN_TK = 512
SCAN_STEPS = 64


def _layer(x, mod3, p, tabs):
    b, s, _ = x.shape
    za, ga, cm, gb, e = _in_proj(x, mod3, p["norm_w"], p["w_pad"], TOKEN_TILE)

    r, kk, v, bon, w2, akk2, krep2 = _rwkv_prep(
        za, p["conv_a"], p["decay_w0"], p["dup2"], p["iclr_a0"], p["iup2"],
        p["k_k"], p["k_a"], p["r_k"], p["ones64"], TOKEN_TILE)
    parts = _value_parts(b)
    o_s = _rwkv_scan(_to_scan_k(r, parts), _to_scan_k(kk, parts), _to_scan_v(v, parts),
                     _to_scan_k(w2, parts), _to_scan_k(akk2, parts), _to_scan_k(krep2, parts),
                     SCAN_STEPS)
    o2 = _from_scan_v(o_s, b, parts)

    q, k, vv = _mla_prep(cm, tabs, p["mw"], TOKEN_TILE)
    oat = _attention(q, k, vv, ATTN_TQ, min(ATTN_TK, s))

    yc = _sgu(e, p["lnw"], p["lnb"], p["wcat"], p["bias"], TOKEN_TILE)

    return _out_proj(x, mod3, o2, bon, ga, oat, gb, yc, p["gnw"], p["gnb"], p["ones64"],
                     p["wa"], p["wb"], p["wc"], TOKEN_TILE)


def kernel(x_prompt, x_sample, c_prompt, c_sample, norm_w, ada_w, ada_b, w_in, conv_a, decay_w0, decay_up, iclr_a0, iclr_up, k_k, k_a, r_k, gn_w, gn_b, q_norm_w, w_uq, kv_norm_w, w_ukv, qk_q_norm_w, qk_k_norm_w, sgu_ln_w, sgu_ln_b, w_s, b_s, w_out):
    depth = norm_w.shape[0]
    bp = x_prompt.shape[0]
    mod = _ada_mod(jnp.concatenate([c_prompt, c_sample], axis=0), ada_w, ada_b)
    mod = mod.reshape(depth, -1, 3, D_MODEL)
    tabs_p = _rope_tables(x_prompt.shape[1])
    tabs_s = _rope_tables(x_sample.shape[1])
    y_p, y_s = x_prompt, x_sample
    for l in range(depth):
        p = _layer_params(l, norm_w, w_in, conv_a, decay_w0, decay_up, iclr_a0, iclr_up, k_k, k_a, r_k,
                          gn_w, gn_b, q_norm_w, w_uq, kv_norm_w, w_ukv, qk_q_norm_w, qk_k_norm_w,
                          sgu_ln_w, sgu_ln_b, w_s, b_s, w_out)
        y_p = _layer(y_p, mod[l, :bp], p, tabs_p)
        y_s = _layer(y_s, mod[l, bp:], p, tabs_s)
    return (y_p, y_s)
```

```python
import functools
import math

import jax
import jax.numpy as jnp
from jax import lax
from jax.experimental import pallas as pl
from jax.experimental.pallas import tpu as pltpu

F32 = jnp.float32
BF16 = jnp.bfloat16

LANES = 128
SUBLANES = 8

D_MODEL = 1024
H_A = 6
HEAD_A = 64
A_W = H_A * HEAD_A
LORA_W = 64
LORA_A = 64
DECAY_SCALE = math.exp(-0.5)
H_B = 6
D_NOPE = 64
D_ROPE = 32
D_QK = D_NOPE + D_ROPE
D_V = 64
B_W = H_B * D_V
Q_RANK = 192
KV_RANK = 128
ROPE_THETA = 10000.0
C_GROUPS = 4
C_W = 256
C_GW = C_W // C_GROUPS
CHUNK = 128
A_CONV_COLS = 3 * A_W + 2 * LORA_W + 2 * LORA_A
NORM_EPS = 1e-6
GN_EPS = 64e-5

QK_PAD = LANES
QKW = H_B * QK_PAD
CM_W = 512
E_W = 3 * C_W
VMEM_LIMIT = 56 * 1024 * 1024


def _cparams(sem):
    return pltpu.CompilerParams(dimension_semantics=sem, vmem_limit_bytes=VMEM_LIMIT)


def _silu(x):
    return x * jax.nn.sigmoid(x)


def _dot(a, b):
    return jnp.dot(a, b, preferred_element_type=F32)


def _seg_sum(x, ones_bf16):
    hi = x.astype(BF16)
    lo = (x - hi.astype(F32)).astype(BF16)
    return _dot(hi, ones_bf16) + _dot(lo, ones_bf16)


def _block_ones(width, block):
    i = jnp.arange(width) // block
    return (i[:, None] == i[None, :]).astype(BF16)


def _mod_kernel(c_ref, w_ref, b_ref, o_ref):
    sc = _silu(c_ref[...])
    o_ref[0] = _dot(sc.astype(BF16), w_ref[0].astype(BF16)) + b_ref[0]


def _ada_mod(c_all, ada_w, ada_b):
    n_layers, d, d3 = ada_w.shape
    nb = c_all.shape[0]
    tn = 768
    return pl.pallas_call(
        _mod_kernel,
        grid=(n_layers, d3 // tn),
        in_specs=[
            pl.BlockSpec((nb, d), lambda l, j: (0, 0)),
            pl.BlockSpec((1, d, tn), lambda l, j: (l, 0, j)),
            pl.BlockSpec((1, 1, tn), lambda l, j: (l, 0, j)),
        ],
        out_specs=pl.BlockSpec((1, nb, tn), lambda l, j: (l, 0, j)),
        out_shape=jax.ShapeDtypeStruct((n_layers, nb, d3), F32),
        compiler_params=_cparams(("parallel", "parallel")),
        name="ada_mod",
    )(c_all, ada_w, ada_b.reshape(n_layers, 1, d3))


IN_GROUPS = (A_CONV_COLS, A_W, CM_W, B_W, E_W)


def _inproj_kernel(x_ref, mod_ref, nw_ref, w_ref, za_ref, ga_ref, cm_ref, gb_ref, e_ref):
    x = x_ref[0]
    ms = jnp.mean(x * x, axis=-1, keepdims=True)
    y = x * lax.rsqrt(ms + NORM_EPS) * nw_ref[...]
    h = (y * (1.0 + mod_ref[0, 1:2, :]) + mod_ref[0, 0:1, :]).astype(BF16)
    off = 0
    for ref, width in zip((za_ref, ga_ref, cm_ref, gb_ref, e_ref), IN_GROUPS):
        ref[0] = _dot(h, w_ref[:, off:off + width])
        off += width


def _in_proj(x, mod3, norm_w, w_pad, tm):
    b, s, d = x.shape
    wtot = w_pad.shape[1]
    tok = lambda width: pl.BlockSpec((1, tm, width), lambda bi, i: (bi, i, 0))
    return pl.pallas_call(
        _inproj_kernel,
        grid=(b, s // tm),
        in_specs=[
            tok(d),
            pl.BlockSpec((1, 3, d), lambda bi, i: (bi, 0, 0)),
            pl.BlockSpec((1, d), lambda bi, i: (0, 0)),
            pl.BlockSpec((d, wtot), lambda bi, i: (0, 0)),
        ],
        out_specs=[tok(wd) for wd in IN_GROUPS],
        out_shape=[jax.ShapeDtypeStruct((b, s, wd), F32) for wd in IN_GROUPS],
        compiler_params=_cparams(("parallel", "parallel")),
        name="in_proj",
    )(x, mod3, norm_w.reshape(1, d), w_pad)


def _rwkv_prep_kernel(z_ref, zp_ref, zn_ref, conv_ref, dw0_ref, dup_ref, ia0_ref, iup_ref,
                      kkw_ref, ka_ref, rk_ref, ones_ref,
                      r_ref, kk_ref, v_ref, bon_ref, lw_ref, akk_ref, krep_ref, *, ts):
    i = pl.program_id(1)
    last = pl.num_programs(1) - 1
    z = z_ref[0]
    prev_row = jnp.where(i > 0, zp_ref[0, SUBLANES - 1:SUBLANES, :], 0.0)
    next_row = jnp.where(i < last, zn_ref[0, 0:1, :], 0.0)
    row = lax.broadcasted_iota(jnp.int32, z.shape, 0)
    z_prev = jnp.where(row == 0, prev_row, pltpu.roll(z, 1, 0))
    z_next = jnp.where(row == ts - 1, next_row, pltpu.roll(z, ts - 1, 0))
    za = z_prev * conv_ref[0:1, :] + z * conv_ref[1:2, :] + z_next * conv_ref[2:3, :]

    r = za[:, 0:A_W]
    k = za[:, A_W:2 * A_W]
    v = za[:, 2 * A_W:3 * A_W]
    low = za[:, 3 * A_W:3 * A_W + 2 * LANES]
    ones = ones_ref[...]

    kx = k * kkw_ref[...]
    nrm = jnp.sqrt(_seg_sum(kx * kx, ones))
    kk = kx / jnp.maximum(nrm, 1e-12)
    bonus = _seg_sum(r * k * rk_ref[...], ones) * v

    r_ref[0] = r
    kk_ref[0] = kk
    v_ref[0] = v
    bon_ref[0] = bonus

    tl = jnp.tanh(low[:, 0:LANES]).astype(BF16)
    la = low[:, LANES:2 * LANES].astype(BF16)
    lane = lax.broadcasted_iota(jnp.int32, tl.shape, 1)
    for d in range(2):
        sel = (lane // LORA_W) == d
        dec = _dot(jnp.where(sel, tl, 0), dup_ref[d]) + dw0_ref[d:d + 1, :]
        a = jax.nn.sigmoid(_dot(jnp.where(sel, la, 0), iup_ref[d]) + ia0_ref[d:d + 1, :])
        lw_ref[d, 0] = -DECAY_SCALE * jax.nn.sigmoid(dec)
        akk_ref[d, 0] = a * kk
        krep_ref[d, 0] = k * (1.0 + (a - 1.0) * ka_ref[...])


def _rwkv_prep(za, conv_a, decay_w0, decay_up2, iclr_a0, iclr_up2, k_k, k_a, r_k, ones64, ts):
    b, s, wz = za.shape
    nsub = ts // SUBLANES
    tok = pl.BlockSpec((1, ts, A_W), lambda bi, i: (bi, i, 0))
    tok2 = pl.BlockSpec((2, 1, ts, A_W), lambda bi, i: (0, bi, i, 0))
    full = lambda shape: pl.BlockSpec(shape, lambda bi, i: (0,) * len(shape))
    o1 = jax.ShapeDtypeStruct((b, s, A_W), F32)
    o2 = jax.ShapeDtypeStruct((2, b, s, A_W), F32)
    return pl.pallas_call(
        functools.partial(_rwkv_prep_kernel, ts=ts),
        grid=(b, s // ts),
        in_specs=[
            pl.BlockSpec((1, ts, wz), lambda bi, i: (bi, i, 0)),
            pl.BlockSpec((1, SUBLANES, wz), lambda bi, i: (bi, jnp.maximum(i * nsub - 1, 0), 0)),
            pl.BlockSpec((1, SUBLANES, wz),
                         lambda bi, i: (bi, jnp.minimum((i + 1) * nsub, s // SUBLANES - 1), 0)),
            full((3, wz)), full((2, A_W)), full((2, LANES, A_W)), full((2, A_W)), full((2, LANES, A_W)),
            full((1, A_W)), full((1, A_W)), full((1, A_W)), full((A_W, A_W)),
        ],
        out_specs=[tok, tok, tok, tok, tok2, tok2, tok2],
        out_shape=[o1, o1, o1, o1, o2, o2, o2],
        compiler_params=_cparams(("parallel", "parallel")),
        name="rwkv_prep",
    )(za, za, za, conv_a, decay_w0, decay_up2, iclr_a0, iclr_up2,
      k_k.reshape(1, A_W), k_a.reshape(1, A_W), r_k.reshape(1, A_W), ones64)


SCAN_CHUNK = 64


def _dot_nt(a, b):
    return lax.dot_general(a, b, (((1,), (1,)), ((), ())), preferred_element_type=F32)


def _dot_tn(a, b):
    return lax.dot_general(a, b, (((0,), (0,)), ((), ())), preferred_element_type=F32)


def _split3(x):
    hi = x.astype(BF16)
    rem = x - hi.astype(F32)
    mid = rem.astype(BF16)
    lo = (rem - mid.astype(F32)).astype(BF16)
    return hi, mid, lo


def _bf(t):
    return t.astype(BF16)


def _unit_tri_inverses(mats, eye, m16, m32):
    d = [jnp.where(m16, a, 0.0) for a in mats]
    db = [_bf(t) for t in d]
    d2b = [_bf(_dot(t, t)) for t in db]
    d4b = [_bf(_dot(t, t)) for t in d2b]
    d8b = [_bf(_dot(t, t)) for t in d4b]
    x = [eye - t for t in d]
    for powers in (d2b, d4b, d8b):
        x = [xi + _dot(_bf(xi), pw) for xi, pw in zip(x, powers)]
    for keep in (m32 & ~m16, ~m32):
        off = [_bf(jnp.where(keep, a, 0.0)) for a in mats]
        xb = [_bf(xi) for xi in x]
        xo = [_bf(_dot(b, o)) for b, o in zip(xb, off)]
        x = [xi - _dot(t, b) for xi, t, b in zip(x, xo, xb)]
    return x


def _scan_kernel(r_ref, kk_ref, v_ref, lw_ref, a_ref, k_ref, o_ref, s_ref, *, nch):
    fwd = pl.program_id(1) == 0

    @pl.when(pl.program_id(2) == 0)
    def _init():
        s_ref[...] = jnp.zeros_like(s_ref)

    c_len = SCAN_CHUNK
    row = lax.broadcasted_iota(jnp.int32, (LANES, LANES), 0)
    col = lax.broadcasted_iota(jnp.int32, (LANES, LANES), 1)
    same = lambda n: (row // n) == (col // n)
    m16, m32, m64 = same(16), same(32), same(c_len)
    sign = jnp.where(fwd, 1, -1)
    strict = ((row - col) * sign > 0) & m64
    incl = ((row - col) * sign >= 0) & m64
    eye = (row == col).astype(F32)
    tr = lax.broadcasted_iota(jnp.int32, (c_len, 3 * c_len), 0)
    tc = lax.broadcasted_iota(jnp.int32, (c_len, 3 * c_len), 1) % c_len
    tri3 = jnp.where((tr - tc) * sign >= 0, 1.0, 0.0).astype(BF16)
    head0 = lax.broadcasted_iota(jnp.int32, (c_len, LANES), 1) < HEAD_A

    def stack(x):
        return jnp.concatenate([jnp.where(head0, x, 0.0), jnp.where(head0, 0.0, x)], axis=0)

    def unstack(xs):
        return xs[0:c_len] + xs[c_len:2 * c_len]

    n_pairs = H_A // 2
    combos = [(c, p) for c in range(nch) for p in range(n_pairs)]
    rows = [pl.ds(pl.multiple_of(jnp.where(fwd, c, nch - 1 - c) * c_len, c_len), c_len)
            for c in range(nch)]
    lanes = [slice(p * LANES, (p + 1) * LANES) for p in range(n_pairs)]

    lw = [lw_ref[0, 0, rows[c], lanes[p]] for c, p in combos]
    cum = [_dot(tri3, jnp.concatenate(_split3(x), axis=0)) for x in lw]
    gmat, kqs_s, rqs_s, vs_b, aip_b, bip_b, dec = [], [], [], [], [], [], []
    for (c, p), lw_i, cum_i in zip(combos, lw, cum):
        r = r_ref[0, rows[c], lanes[p]]
        kk = kk_ref[0, rows[c], lanes[p]]
        a = a_ref[0, 0, rows[c], lanes[p]]
        k = k_ref[0, 0, rows[c], lanes[p]]
        cum_x = cum_i - lw_i
        mid = cum_i[c_len // 2:c_len // 2 + 1, :]
        tot = jnp.sum(lw_i, axis=0, keepdims=True)
        e_in = jnp.exp(mid - cum_i)
        e_tot = jnp.exp(tot - cum_i)
        kq, rq = kk * jnp.exp(cum_x - mid), r * jnp.exp(cum_i - mid)
        gmat.append(_dot_nt(_bf(jnp.concatenate([stack(kq), stack(rq)], axis=0)),
                            _bf(jnp.concatenate([stack(a * e_in), stack(k * e_in)], axis=0))))
        kqs_s.append(stack(kk * jnp.exp(cum_x)))
        rqs_s.append(stack(r * jnp.exp(cum_i)))
        vs_b.append(_bf(stack(v_ref[0, rows[c], lanes[p]])))
        aip_b.append(_bf(stack(a * e_tot)))
        bip_b.append(_bf(stack(k * e_tot)))
        dec.append(jnp.exp(tot))
    tinv = _unit_tri_inverses([jnp.where(strict, g[0:LANES, 0:LANES], 0.0) for g in gmat],
                              eye, m16, m32)
    kbv = [_dot(_bf(jnp.where(strict, g[0:LANES, LANES:2 * LANES], 0.0)), v)
           for g, v in zip(gmat, vs_b)]
    wtk = [_bf(_dot(_bf(t), _bf(jnp.concatenate([x, q], axis=1))))
           for t, x, q in zip(tinv, kbv, kqs_s)]
    rav = [_dot(_bf(jnp.where(incl, g[LANES:2 * LANES, 0:LANES], 0.0)), w) for g, w in zip(gmat, wtk)]
    rbv = [_dot(_bf(jnp.where(incl, g[LANES:2 * LANES, LANES:2 * LANES], 0.0)), v)
           for g, v in zip(gmat, vs_b)]
    o0 = [b - a[:, 0:LANES] for a, b in zip(rav, rbv)]
    qmat = [_bf(q - a[:, LANES:2 * LANES]) for a, q in zip(rav, rqs_s)]
    wa = [_dot_tn(w, a) for w, a in zip(wtk, aip_b)]
    nmat = [_dot_tn(v, b) - t[0:LANES] for v, b, t in zip(vs_b, bip_b, wa)]
    mmat = [_bf(t[LANES:2 * LANES]) for t in wa]

    state = [s_ref[p] for p in range(n_pairs)]
    for c in range(nch):
        sb = [_bf(s) for s in state]
        for p in range(n_pairs):
            i = c * n_pairs + p
            o_ref[0, 0, rows[c], lanes[p]] = unstack(_dot_nt(qmat[i], sb[p]) + o0[i])
        state = [state[p] * dec[c * n_pairs + p] - _dot(sb[p], mmat[c * n_pairs + p])
                 + nmat[c * n_pairs + p] for p in range(n_pairs)]
    for p in range(n_pairs):
        s_ref[p] = state[p]


def _rwkv_scan(r, kk, v, lw2, a2, k2, tile):
    b, s, _ = r.shape
    nt = s // tile
    tblk = lambda g, i: i + g * (nt - 1 - 2 * i)
    shared = pl.BlockSpec((1, tile, A_W), lambda bi, g, i: (bi, tblk(g, i), 0))
    perdir = pl.BlockSpec((1, 1, tile, A_W), lambda bi, g, i: (g, bi, tblk(g, i), 0))
    return pl.pallas_call(
        functools.partial(_scan_kernel, nch=tile // SCAN_CHUNK),
        grid=(b, 2, nt),
        in_specs=[shared, shared, shared, perdir, perdir, perdir],
        out_specs=perdir,
        out_shape=jax.ShapeDtypeStruct((2, b, s, A_W), F32),
        scratch_shapes=[pltpu.VMEM((H_A // 2, LANES, LANES), F32)],
        compiler_params=_cparams(("parallel", "arbitrary", "arbitrary")),
        name="rwkv_scan",
    )(r, kk, v, lw2, a2, k2)


def _rope(x, c_t, s1_t, s2_t):
    width = x.shape[-1]
    half = D_ROPE // 2
    return x * c_t + pltpu.roll(x, width - half, 1) * s1_t + pltpu.roll(x, half, 1) * s2_t


def _mla_prep_kernel(cm_ref, c_ref, s1_ref, s2_ref, qnw_ref, wuq_ref, kvnw_ref, wuk_ref, wuv_ref,
                     place_ref, qkq_ref, qkk_ref, ones_ref, q_ref, k_ref, v_ref):
    cm = cm_ref[0]
    ones = ones_ref[...]
    c_t = jnp.concatenate([c_ref[...]] * H_B, axis=1)
    s1_t = jnp.concatenate([s1_ref[...]] * H_B, axis=1)
    s2_t = jnp.concatenate([s2_ref[...]] * H_B, axis=1)

    cq = cm[:, 0:2 * LANES]
    msq = jnp.sum(cq * cq, axis=-1, keepdims=True) * (1.0 / Q_RANK)
    cqn = (cq * lax.rsqrt(msq + NORM_EPS) * qnw_ref[...]).astype(BF16)
    q = _dot(cqn, wuq_ref[...])
    q = q * lax.rsqrt(_seg_sum(q * q, ones) * (1.0 / D_QK) + NORM_EPS) * qkq_ref[...]
    q_ref[0] = _rope(q, c_t, s1_t, s2_t).astype(BF16)

    ckv = cm[:, 2 * LANES:3 * LANES]
    mskv = jnp.mean(ckv * ckv, axis=-1, keepdims=True)
    ckvn = (ckv * lax.rsqrt(mskv + NORM_EPS) * kvnw_ref[...]).astype(BF16)
    v_ref[0] = _dot(ckvn, wuv_ref[...]).astype(BF16)

    kr = cm[:, 3 * LANES:4 * LANES]
    place = place_ref[...]
    kr_hi = kr.astype(BF16)
    rem = kr - kr_hi.astype(F32)
    kr_mid = rem.astype(BF16)
    kr_lo = (rem - kr_mid.astype(F32)).astype(BF16)
    k = _dot(ckvn, wuk_ref[...]) + _dot(kr_hi, place) + _dot(kr_mid, place) + _dot(kr_lo, place)
    k = k * lax.rsqrt(_seg_sum(k * k, ones) * (1.0 / D_QK) + NORM_EPS) * qkk_ref[...]
    k_ref[0] = _rope(k, c_t, s1_t, s2_t).astype(BF16)


def _mla_prep(cm, tabs, mw, ts):
    b, s, _ = cm.shape
    full = lambda shape: pl.BlockSpec(shape, lambda bi, i: (0,) * len(shape))
    tab = pl.BlockSpec((ts, LANES), lambda bi, i: (i, 0))
    out = pl.BlockSpec((1, ts, QKW), lambda bi, i: (bi, i, 0))
    osh = jax.ShapeDtypeStruct((b, s, QKW), BF16)
    return pl.pallas_call(
        _mla_prep_kernel,
        grid=(b, s // ts),
        in_specs=[
            pl.BlockSpec((1, ts, CM_W), lambda bi, i: (bi, i, 0)), tab, tab, tab,
            full((1, 2 * LANES)), full((2 * LANES, QKW)), full((1, LANES)), full((LANES, QKW)),
            full((LANES, QKW)), full((LANES, QKW)), full((1, QKW)), full((1, QKW)), full((QKW, QKW)),
        ],
        out_specs=[out, out, out],
        out_shape=[osh, osh, osh],
        compiler_params=_cparams(("parallel", "parallel")),
        name="mla_prep",
    )(cm, *tabs, mw["qnw"], mw["wuq"], mw["kvnw"], mw["wuk"], mw["wuv"], mw["place"],
      mw["qkq"], mw["qkk"], mw["ones128"])


def _rope_tables(s):
    half = D_ROPE // 2
    freqs = ROPE_THETA ** (-jnp.arange(half, dtype=F32) / half)
    ang = jnp.arange(s, dtype=F32)[:, None] * freqs[None, :]
    cos, sin = jnp.cos(ang), jnp.sin(ang)
    z = lambda n: jnp.zeros((s, n), F32)
    c_t = jnp.concatenate([jnp.ones((s, D_NOPE), F32), cos, cos, z(QK_PAD - D_QK)], axis=1)
    s1_t = jnp.concatenate([z(D_NOPE), -sin, z(half), z(QK_PAD - D_QK)], axis=1)
    s2_t = jnp.concatenate([z(D_NOPE), z(half), sin, z(QK_PAD - D_QK)], axis=1)
    return c_t, s1_t, s2_t


def _attn_kernel(q_ref, k_ref, v_ref, o_ref, *, tk):
    tq = q_ref.shape[1]
    nk = k_ref.shape[1] // tk
    out = None
    for h in range(2):
        hs = slice(h * QK_PAD, (h + 1) * QK_PAD)
        q = q_ref[0, :, hs]

        def body(j, carry, hs=hs, q=q):
            m, l, acc = carry
            rows = pl.ds(pl.multiple_of(j * tk, tk), tk)
            s = lax.dot_general(q, k_ref[0, rows, hs], (((1,), (1,)), ((), ())),
                                preferred_element_type=F32)
            m_new = jnp.maximum(m, jnp.max(s, axis=-1, keepdims=True))
            alpha = jnp.exp(m - m_new)
            p = jnp.exp(s - m_new)
            l = alpha * l + jnp.sum(p, axis=-1, keepdims=True)
            acc = alpha * acc + _dot(p.astype(BF16), v_ref[0, rows, hs])
            return m_new, l, acc

        m0 = jnp.full((tq, 1), -1e30, F32)
        l0 = jnp.zeros((tq, 1), F32)
        a0 = jnp.zeros((tq, QK_PAD), F32)
        _, l, acc = lax.fori_loop(0, nk, body, (m0, l0, a0))
        part = acc / l
        out = part if out is None else out + part
    o_ref[0] = out


def _attention(q, k, v, tq, tk):
    b, s, _ = q.shape
    pair_w = 2 * QK_PAD
    return pl.pallas_call(
        functools.partial(_attn_kernel, tk=tk),
        grid=(b, H_B // 2, s // tq),
        in_specs=[
            pl.BlockSpec((1, tq, pair_w), lambda bi, hp, i: (bi, i, hp)),
            pl.BlockSpec((1, s, pair_w), lambda bi, hp, i: (bi, 0, hp)),
            pl.BlockSpec((1, s, pair_w), lambda bi, hp, i: (bi, 0, hp)),
        ],
        out_specs=pl.BlockSpec((1, tq, 2 * D_V), lambda bi, hp, i: (bi, i, hp)),
        out_shape=jax.ShapeDtypeStruct((b, s, B_W), F32),
        compiler_params=_cparams(("parallel", "parallel", "parallel")),
        name="mla_attention",
    )(q, k, v)


def _sgu_kernel(e_ref, lnw_ref, lnb_ref, wcat_ref, bias_ref, y_ref, *, ts):
    e = e_ref[0]
    u = e[:, 0:C_W]
    v = e[:, C_W:2 * C_W]
    g = e[:, 2 * C_W:3 * C_W]
    mu = jnp.mean(v, axis=-1, keepdims=True)
    d = v - mu
    var = jnp.mean(d * d, axis=-1, keepdims=True)
    vn = d * lax.rsqrt(var + NORM_EPS) * lnw_ref[...] + lnb_ref[...]
    group = lax.broadcasted_iota(jnp.int32, (CHUNK, C_W), 1) // C_GW
    wcat = wcat_ref[...]
    bias = bias_ref[...]
    for c in range(ts // CHUNK):
        rows = slice(c * CHUNK, (c + 1) * CHUNK)
        vc = vn[rows, :]
        stack = jnp.concatenate([jnp.where(group == gi, vc, 0.0) for gi in range(C_GROUPS)], axis=0)
        sv = _dot(wcat, stack.astype(BF16)) + bias
        y_ref[0, rows, :] = u[rows, :] * sv * _silu(g[rows, :])


def _sgu(e, lnw, lnb, wcat, bias, ts):
    b, s, _ = e.shape
    full = lambda shape: pl.BlockSpec(shape, lambda bi, i: (0,) * len(shape))
    return pl.pallas_call(
        functools.partial(_sgu_kernel, ts=ts),
        grid=(b, s // ts),
        in_specs=[pl.BlockSpec((1, ts, E_W), lambda bi, i: (bi, i, 0)),
                  full((1, C_W)), full((1, C_W)), full((CHUNK, C_GROUPS * CHUNK)), full((CHUNK, C_W))],
        out_specs=pl.BlockSpec((1, ts, C_W), lambda bi, i: (bi, i, 0)),
        out_shape=jax.ShapeDtypeStruct((b, s, C_W), F32),
        compiler_params=_cparams(("parallel", "parallel")),
        name="sgu",
    )(e, lnw, lnb, wcat, bias)


def _out_kernel(x_ref, mod_ref, o2_ref, bon_ref, ga_ref, oat_ref, gb_ref, yc_ref,
                gnw_ref, gnb_ref, ones_ref, wa_ref, wb_ref, wc_ref, out_ref):
    ones = ones_ref[...]
    o = o2_ref[0, 0] + o2_ref[1, 0]
    mu = _seg_sum(o, ones) * (1.0 / HEAD_A)
    d = o - mu
    var = _seg_sum(d * d, ones) * (1.0 / HEAD_A)
    on = d * lax.rsqrt(var + GN_EPS)
    ya = (on * gnw_ref[...] + gnb_ref[...] + bon_ref[0]) * _silu(ga_ref[0])
    yb = oat_ref[0] * _silu(gb_ref[0])
    y = (_dot(ya.astype(BF16), wa_ref[...]) + _dot(yb.astype(BF16), wb_ref[...])
         + _dot(yc_ref[0].astype(BF16), wc_ref[...]))
    out_ref[0] = x_ref[0] + mod_ref[0, 2:3, :] * y


def _out_proj(x, mod3, o2, bon, ga, oat, gb, yc, gnw, gnb, ones64, wa, wb, wc, tm):
    b, s, d = x.shape
    tok = lambda width: pl.BlockSpec((1, tm, width), lambda bi, i: (bi, i, 0))
    full = lambda shape: pl.BlockSpec(shape, lambda bi, i: (0,) * len(shape))
    return pl.pallas_call(
        _out_kernel,
        grid=(b, s // tm),
        in_specs=[
            tok(d), pl.BlockSpec((1, 3, d), lambda bi, i: (bi, 0, 0)),
            pl.BlockSpec((2, 1, tm, A_W), lambda bi, i: (0, bi, i, 0)),
            tok(A_W), tok(A_W), tok(B_W), tok(B_W), tok(C_W),
            full((1, A_W)), full((1, A_W)), full((A_W, A_W)),
            full((A_W, d)), full((B_W, d)), full((C_W, d)),
        ],
        out_specs=tok(d),
        out_shape=jax.ShapeDtypeStruct((b, s, d), F32),
        compiler_params=_cparams(("parallel", "parallel")),
        name="out_proj",
    )(x, mod3, o2, bon, ga, oat, gb, yc, gnw, gnb, ones64, wa, wb, wc)


def _pad_cols(w, width):
    return jnp.pad(w, ((0, 0), (0, width - w.shape[1])))


def _per_head_cols(w, real, padded):
    kdim = w.shape[0]
    w = w.reshape(kdim, -1, real)
    return jnp.pad(w, ((0, 0), (0, 0), (0, padded - real))).reshape(kdim, -1)


def _layer_params(l, norm_w, w_in, conv_a, decay_w0, decay_up, iclr_a0, iclr_up, k_k, k_a, r_k,
                  gn_w, gn_b, q_norm_w, w_uq, kv_norm_w, w_ukv, qk_q_norm_w, qk_k_norm_w,
                  sgu_ln_w, sgu_ln_b, w_s, b_s, w_out):
    wi = w_in[l]
    o_ga = A_CONV_COLS
    o_cq = o_ga + A_W
    o_ckv = o_cq + Q_RANK
    o_kr = o_ckv + KV_RANK
    o_gb = o_kr + D_ROPE
    o_e = o_gb + B_W
    w_pad = jnp.concatenate([
        wi[:, :o_cq],
        _pad_cols(wi[:, o_cq:o_ckv], 2 * LANES),
        wi[:, o_ckv:o_kr],
        _pad_cols(wi[:, o_kr:o_gb], LANES),
        wi[:, o_gb:],
    ], axis=1).astype(BF16)

    zeros64 = jnp.zeros((LORA_W, A_W), F32)
    dup2 = jnp.stack([jnp.concatenate([decay_up[l, 0], zeros64], 0),
                      jnp.concatenate([zeros64, decay_up[l, 1]], 0)]).astype(BF16)
    iup2 = jnp.stack([jnp.concatenate([iclr_up[l, 0], zeros64], 0),
                      jnp.concatenate([zeros64, iclr_up[l, 1]], 0)]).astype(BF16)

    scale = D_QK ** -0.5
    wukv = w_ukv[l].reshape(KV_RANK, H_B, D_NOPE + D_V)
    wuk = jnp.pad(wukv[:, :, :D_NOPE], ((0, 0), (0, 0), (0, QK_PAD - D_NOPE))).reshape(KV_RANK, QKW)
    wv = wukv[:, :, D_NOPE:].reshape(KV_RANK, H_B // 2, 2, D_V)
    zv = jnp.zeros((KV_RANK, H_B // 2, D_V), F32)
    wuv = jnp.stack([wv[:, :, 0], zv, zv, wv[:, :, 1]], axis=2).reshape(KV_RANK, QKW)
    lane = jnp.arange(QKW)
    place = ((lane[None, :] % QK_PAD) == (D_NOPE + jnp.arange(LANES)[:, None])) & (jnp.arange(LANES)[:, None] < D_ROPE)
    head_w = lambda w: jnp.tile(jnp.pad(w, (0, QK_PAD - D_QK)), H_B).reshape(1, QKW)
    mw = dict(
        qnw=jnp.pad(q_norm_w[l], (0, 2 * LANES - Q_RANK)).reshape(1, 2 * LANES),
        wuq=jnp.pad(_per_head_cols(w_uq[l], D_QK, QK_PAD), ((0, 2 * LANES - Q_RANK), (0, 0))).astype(BF16),
        kvnw=kv_norm_w[l].reshape(1, KV_RANK),
        wuk=wuk.astype(BF16),
        wuv=wuv.astype(BF16),
        place=place.astype(BF16),
        qkq=head_w(qk_q_norm_w[l]) * scale,
        qkk=head_w(qk_k_norm_w[l]),
        ones128=_block_ones(QKW, QK_PAD),
    )
    wcat = jnp.transpose(w_s[l], (1, 0, 2)).reshape(CHUNK, C_GROUPS * CHUNK).astype(BF16)
    bias = jnp.repeat(b_s[l].T, C_GW, axis=1)
    wo = w_out[l].astype(BF16)
    return dict(
        norm_w=norm_w[l], w_pad=w_pad, conv_a=conv_a[l], decay_w0=decay_w0[l], dup2=dup2,
        iclr_a0=iclr_a0[l], iup2=iup2, k_k=k_k[l], k_a=k_a[l], r_k=r_k[l].reshape(A_W),
        gnw=gn_w[l].reshape(1, A_W), gnb=gn_b[l].reshape(1, A_W), ones64=_block_ones(A_W, HEAD_A),
        mw=mw, lnw=sgu_ln_w[l].reshape(1, C_W), lnb=sgu_ln_b[l].reshape(1, C_W), wcat=wcat, bias=bias,
        wa=wo[:A_W], wb=wo[A_W:A_W + B_W], wc=wo[A_W + B_W:],
    )


TOKEN_TILE = 256
ATTN_TQ = 256
ATTN_TK = 512


def _layer(x, mod3, p, tabs):
    b, s, _ = x.shape
    za, ga, cm, gb, e = _in_proj(x, mod3, p["norm_w"], p["w_pad"], TOKEN_TILE)

    r, kk, v, bon, lw2, akk2, krep2 = _rwkv_prep(
        za, p["conv_a"], p["decay_w0"], p["dup2"], p["iclr_a0"], p["iup2"],
        p["k_k"], p["k_a"], p["r_k"], p["ones64"], TOKEN_TILE)
    o2 = _rwkv_scan(r, kk, v, lw2, akk2, krep2, TOKEN_TILE)

    q, k, vv = _mla_prep(cm, tabs, p["mw"], TOKEN_TILE)
    oat = _attention(q, k, vv, ATTN_TQ, min(ATTN_TK, s))

    yc = _sgu(e, p["lnw"], p["lnb"], p["wcat"], p["bias"], TOKEN_TILE)

    return _out_proj(x, mod3, o2, bon, ga, oat, gb, yc, p["gnw"], p["gnb"], p["ones64"],
                     p["wa"], p["wb"], p["wc"], TOKEN_TILE)


def kernel(x_prompt, x_sample, c_prompt, c_sample, norm_w, ada_w, ada_b, w_in, conv_a, decay_w0, decay_up, iclr_a0, iclr_up, k_k, k_a, r_k, gn_w, gn_b, q_norm_w, w_uq, kv_norm_w, w_ukv, qk_q_norm_w, qk_k_norm_w, sgu_ln_w, sgu_ln_b, w_s, b_s, w_out):
    depth = norm_w.shape[0]
    bp = x_prompt.shape[0]
    mod = _ada_mod(jnp.concatenate([c_prompt, c_sample], axis=0), ada_w, ada_b)
    mod = mod.reshape(depth, -1, 3, D_MODEL)
    tabs_p = _rope_tables(x_prompt.shape[1])
    tabs_s = _rope_tables(x_sample.shape[1])
    y_p, y_s = x_prompt, x_sample
    for l in range(depth):
        p = _layer_params(l, norm_w, w_in, conv_a, decay_w0, decay_up, iclr_a0, iclr_up, k_k, k_a, r_k,
                          gn_w, gn_b, q_norm_w, w_uq, kv_norm_w, w_ukv, qk_q_norm_w, qk_k_norm_w,
                          sgu_ln_w, sgu_ln_b, w_s, b_s, w_out)
        y_p = _layer(y_p, mod[l, :bp], p, tabs_p)
        y_s = _layer(y_s, mod[l, bp:], p, tabs_s)
    return (y_p, y_s)
```

```python
import functools
import math

import jax
import jax.numpy as jnp
from jax import lax
from jax.experimental import pallas as pl
from jax.experimental.pallas import tpu as pltpu

F32 = jnp.float32
BF16 = jnp.bfloat16

LANES = 128
SUBLANES = 8

D_MODEL = 1024
H_A = 6
HEAD_A = 64
A_W = H_A * HEAD_A
LORA_W = 64
LORA_A = 64
DECAY_SCALE = math.exp(-0.5)
H_B = 6
D_NOPE = 64
D_ROPE = 32
D_QK = D_NOPE + D_ROPE
D_V = 64
B_W = H_B * D_V
Q_RANK = 192
KV_RANK = 128
ROPE_THETA = 10000.0
C_GROUPS = 4
C_W = 256
C_GW = C_W // C_GROUPS
CHUNK = 128
A_CONV_COLS = 3 * A_W + 2 * LORA_W + 2 * LORA_A
NORM_EPS = 1e-6
GN_EPS = 64e-5

QK_PAD = LANES
ONES_LANE_EVEN = D_V
ONES_LANE_ODD = 0
LOG2E = math.log2(math.e)
QKW = H_B * QK_PAD
CM_W = 512
E_W = 3 * C_W
VMEM_LIMIT = 56 * 1024 * 1024


def _cparams(sem):
    return pltpu.CompilerParams(dimension_semantics=sem, vmem_limit_bytes=VMEM_LIMIT)


def _silu(x):
    return x * jax.nn.sigmoid(x)


def _dot(a, b):
    return jnp.dot(a, b, preferred_element_type=F32)


def _seg_sum(x, ones_bf16):
    hi = x.astype(BF16)
    lo = (x - hi.astype(F32)).astype(BF16)
    return _dot(hi, ones_bf16) + _dot(lo, ones_bf16)


def _block_ones(width, block):
    i = jnp.arange(width) // block
    return (i[:, None] == i[None, :]).astype(BF16)


def _mod_kernel(c_ref, w_ref, b_ref, o_ref):
    sc = _silu(c_ref[...])
    o_ref[0] = _dot(sc.astype(BF16), w_ref[0].astype(BF16)) + b_ref[0]


def _ada_mod(c_all, ada_w, ada_b):
    n_layers, d, d3 = ada_w.shape
    nb = c_all.shape[0]
    tn = 768
    return pl.pallas_call(
        _mod_kernel,
        grid=(n_layers, d3 // tn),
        in_specs=[
            pl.BlockSpec((nb, d), lambda l, j: (0, 0)),
            pl.BlockSpec((1, d, tn), lambda l, j: (l, 0, j)),
            pl.BlockSpec((1, 1, tn), lambda l, j: (l, 0, j)),
        ],
        out_specs=pl.BlockSpec((1, nb, tn), lambda l, j: (l, 0, j)),
        out_shape=jax.ShapeDtypeStruct((n_layers, nb, d3), F32),
        compiler_params=_cparams(("parallel", "parallel")),
        name="ada_mod",
    )(c_all, ada_w, ada_b.reshape(n_layers, 1, d3))


IN_GROUPS = (A_CONV_COLS, A_W, CM_W, B_W, E_W)


def _inproj_kernel(x_ref, mod_ref, nw_ref, w_ref, za_ref, ga_ref, cm_ref, gb_ref, e_ref):
    x = x_ref[0]
    ms = jnp.mean(x * x, axis=-1, keepdims=True)
    y = x * lax.rsqrt(ms + NORM_EPS) * nw_ref[...]
    h = (y * (1.0 + mod_ref[0, 1:2, :]) + mod_ref[0, 0:1, :]).astype(BF16)
    off = 0
    for ref, width in zip((za_ref, ga_ref, cm_ref, gb_ref, e_ref), IN_GROUPS):
        ref[0] = _dot(h, w_ref[:, off:off + width])
        off += width


def _in_proj(x, mod3, norm_w, w_pad, tm):
    b, s, d = x.shape
    wtot = w_pad.shape[1]
    tok = lambda width: pl.BlockSpec((1, tm, width), lambda bi, i: (bi, i, 0))
    return pl.pallas_call(
        _inproj_kernel,
        grid=(b, s // tm),
        in_specs=[
            tok(d),
            pl.BlockSpec((1, 3, d), lambda bi, i: (bi, 0, 0)),
            pl.BlockSpec((1, d), lambda bi, i: (0, 0)),
            pl.BlockSpec((d, wtot), lambda bi, i: (0, 0)),
        ],
        out_specs=[tok(wd) for wd in IN_GROUPS],
        out_shape=[jax.ShapeDtypeStruct((b, s, wd), F32) for wd in IN_GROUPS],
        compiler_params=_cparams(("parallel", "parallel")),
        name="in_proj",
    )(x, mod3, norm_w.reshape(1, d), w_pad)


def _rwkv_prep_kernel(z_ref, zp_ref, zn_ref, conv_ref, dw0_ref, dup_ref, ia0_ref, iup_ref,
                      kkw_ref, ka_ref, rk_ref, ones_ref,
                      r_ref, kk_ref, v_ref, bon_ref, lw_ref, akk_ref, krep_ref, *, ts):
    i = pl.program_id(1)
    last = pl.num_programs(1) - 1
    z = z_ref[0]
    prev_row = jnp.where(i > 0, zp_ref[0, SUBLANES - 1:SUBLANES, :], 0.0)
    next_row = jnp.where(i < last, zn_ref[0, 0:1, :], 0.0)
    row = lax.broadcasted_iota(jnp.int32, z.shape, 0)
    z_prev = jnp.where(row == 0, prev_row, pltpu.roll(z, 1, 0))
    z_next = jnp.where(row == ts - 1, next_row, pltpu.roll(z, ts - 1, 0))
    za = z_prev * conv_ref[0:1, :] + z * conv_ref[1:2, :] + z_next * conv_ref[2:3, :]

    r = za[:, 0:A_W]
    k = za[:, A_W:2 * A_W]
    v = za[:, 2 * A_W:3 * A_W]
    low = za[:, 3 * A_W:3 * A_W + 2 * LANES]
    ones = ones_ref[...]

    kx = k * kkw_ref[...]
    nrm = jnp.sqrt(_seg_sum(kx * kx, ones))
    kk = kx / jnp.maximum(nrm, 1e-12)
    bonus = _seg_sum(r * k * rk_ref[...], ones) * v

    r_ref[0] = r
    kk_ref[0] = kk
    v_ref[0] = v
    bon_ref[0] = bonus

    tl = jnp.tanh(low[:, 0:LANES]).astype(BF16)
    la = low[:, LANES:2 * LANES].astype(BF16)
    lane = lax.broadcasted_iota(jnp.int32, tl.shape, 1)
    for d in range(2):
        sel = (lane // LORA_W) == d
        dec = _dot(jnp.where(sel, tl, 0), dup_ref[d]) + dw0_ref[d:d + 1, :]
        a = jax.nn.sigmoid(_dot(jnp.where(sel, la, 0), iup_ref[d]) + ia0_ref[d:d + 1, :])
        lw_ref[d, 0] = -DECAY_SCALE * jax.nn.sigmoid(dec)
        akk_ref[d, 0] = a * kk
        krep_ref[d, 0] = k * (1.0 + (a - 1.0) * ka_ref[...])


def _rwkv_prep(za, conv_a, decay_w0, decay_up2, iclr_a0, iclr_up2, k_k, k_a, r_k, ones64, ts):
    b, s, wz = za.shape
    nsub = ts // SUBLANES
    tok = pl.BlockSpec((1, ts, A_W), lambda bi, i: (bi, i, 0))
    tok2 = pl.BlockSpec((2, 1, ts, A_W), lambda bi, i: (0, bi, i, 0))
    full = lambda shape: pl.BlockSpec(shape, lambda bi, i: (0,) * len(shape))
    o1 = jax.ShapeDtypeStruct((b, s, A_W), F32)
    o2 = jax.ShapeDtypeStruct((2, b, s, A_W), F32)
    return pl.pallas_call(
        functools.partial(_rwkv_prep_kernel, ts=ts),
        grid=(b, s // ts),
        in_specs=[
            pl.BlockSpec((1, ts, wz), lambda bi, i: (bi, i, 0)),
            pl.BlockSpec((1, SUBLANES, wz), lambda bi, i: (bi, jnp.maximum(i * nsub - 1, 0), 0)),
            pl.BlockSpec((1, SUBLANES, wz),
                         lambda bi, i: (bi, jnp.minimum((i + 1) * nsub, s // SUBLANES - 1), 0)),
            full((3, wz)), full((2, A_W)), full((2, LANES, A_W)), full((2, A_W)), full((2, LANES, A_W)),
            full((1, A_W)), full((1, A_W)), full((1, A_W)), full((A_W, A_W)),
        ],
        out_specs=[tok, tok, tok, tok, tok2, tok2, tok2],
        out_shape=[o1, o1, o1, o1, o2, o2, o2],
        compiler_params=_cparams(("parallel", "parallel")),
        name="rwkv_prep",
    )(za, za, za, conv_a, decay_w0, decay_up2, iclr_a0, iclr_up2,
      k_k.reshape(1, A_W), k_a.reshape(1, A_W), r_k.reshape(1, A_W), ones64)


SCAN_CHUNK = 64


def _dot_nt(a, b):
    return lax.dot_general(a, b, (((1,), (1,)), ((), ())), preferred_element_type=F32)


def _dot_tn(a, b):
    return lax.dot_general(a, b, (((0,), (0,)), ((), ())), preferred_element_type=F32)


def _split3(x):
    hi = x.astype(BF16)
    rem = x - hi.astype(F32)
    mid = rem.astype(BF16)
    lo = (rem - mid.astype(F32)).astype(BF16)
    return hi, mid, lo


def _bf(t):
    return t.astype(BF16)


def _unit_tri_inverses(mats, eye, m16, m32):
    d = [jnp.where(m16, a, 0.0) for a in mats]
    db = [_bf(t) for t in d]
    d2b = [_bf(_dot(t, t)) for t in db]
    d4b = [_bf(_dot(t, t)) for t in d2b]
    d8b = [_bf(_dot(t, t)) for t in d4b]
    x = [eye - t for t in d]
    for powers in (d2b, d4b, d8b):
        x = [xi + _dot(_bf(xi), pw) for xi, pw in zip(x, powers)]
    for keep in (m32 & ~m16, ~m32):
        off = [_bf(jnp.where(keep, a, 0.0)) for a in mats]
        xb = [_bf(xi) for xi in x]
        xo = [_bf(_dot(b, o)) for b, o in zip(xb, off)]
        x = [xi - _dot(t, b) for xi, t, b in zip(x, xo, xb)]
    return x


def _scan_kernel(r_ref, kk_ref, v_ref, lw_ref, a_ref, k_ref, o_ref, s_ref, *, nch):
    fwd = pl.program_id(1) == 0

    @pl.when(pl.program_id(2) == 0)
    def _init():
        s_ref[...] = jnp.zeros_like(s_ref)

    c_len = SCAN_CHUNK
    row = lax.broadcasted_iota(jnp.int32, (LANES, LANES), 0)
    col = lax.broadcasted_iota(jnp.int32, (LANES, LANES), 1)
    same = lambda n: (row // n) == (col // n)
    m16, m32, m64 = same(16), same(32), same(c_len)
    sign = jnp.where(fwd, 1, -1)
    strict = ((row - col) * sign > 0) & m64
    incl = ((row - col) * sign >= 0) & m64
    eye = (row == col).astype(F32)
    tr = lax.broadcasted_iota(jnp.int32, (c_len, 3 * c_len), 0)
    tc = lax.broadcasted_iota(jnp.int32, (c_len, 3 * c_len), 1) % c_len
    tri3 = jnp.where((tr - tc) * sign >= 0, 1.0, 0.0).astype(BF16)
    head0 = lax.broadcasted_iota(jnp.int32, (c_len, LANES), 1) < HEAD_A

    def stack(x):
        return jnp.concatenate([jnp.where(head0, x, 0.0), jnp.where(head0, 0.0, x)], axis=0)

    def unstack(xs):
        return xs[0:c_len] + xs[c_len:2 * c_len]

    n_pairs = H_A // 2
    combos = [(c, p) for c in range(nch) for p in range(n_pairs)]
    rows = [pl.ds(pl.multiple_of(jnp.where(fwd, c, nch - 1 - c) * c_len, c_len), c_len)
            for c in range(nch)]
    lanes = [slice(p * LANES, (p + 1) * LANES) for p in range(n_pairs)]

    lw = [lw_ref[0, 0, rows[c], lanes[p]] for c, p in combos]
    cum = [_dot(tri3, jnp.concatenate(_split3(x), axis=0)) for x in lw]
    gmat, kqs_s, rqs_s, vs_b, aip_b, bip_b, dec = [], [], [], [], [], [], []
    for (c, p), lw_i, cum_i in zip(combos, lw, cum):
        r = r_ref[0, rows[c], lanes[p]]
        kk = kk_ref[0, rows[c], lanes[p]]
        a = a_ref[0, 0, rows[c], lanes[p]]
        k = k_ref[0, 0, rows[c], lanes[p]]
        cum_x = cum_i - lw_i
        mid = cum_i[c_len // 2:c_len // 2 + 1, :]
        tot = jnp.sum(lw_i, axis=0, keepdims=True)
        e_in = jnp.exp(mid - cum_i)
        e_tot = jnp.exp(tot - cum_i)
        kq, rq = kk * jnp.exp(cum_x - mid), r * jnp.exp(cum_i - mid)
        gmat.append(_dot_nt(_bf(jnp.concatenate([stack(kq), stack(rq)], axis=0)),
                            _bf(jnp.concatenate([stack(a * e_in), stack(k * e_in)], axis=0))))
        kqs_s.append(stack(kk * jnp.exp(cum_x)))
        rqs_s.append(stack(r * jnp.exp(cum_i)))
        vs_b.append(_bf(stack(v_ref[0, rows[c], lanes[p]])))
        aip_b.append(_bf(stack(a * e_tot)))
        bip_b.append(_bf(stack(k * e_tot)))
        dec.append(jnp.exp(tot))
    tinv = _unit_tri_inverses([jnp.where(strict, g[0:LANES, 0:LANES], 0.0) for g in gmat],
                              eye, m16, m32)
    kbv = [_dot(_bf(jnp.where(strict, g[0:LANES, LANES:2 * LANES], 0.0)), v)
           for g, v in zip(gmat, vs_b)]
    wtk = [_bf(_dot(_bf(t), _bf(jnp.concatenate([x, q], axis=1))))
           for t, x, q in zip(tinv, kbv, kqs_s)]
    rav = [_dot(_bf(jnp.where(incl, g[LANES:2 * LANES, 0:LANES], 0.0)), w) for g, w in zip(gmat, wtk)]
    rbv = [_dot(_bf(jnp.where(incl, g[LANES:2 * LANES, LANES:2 * LANES], 0.0)), v)
           for g, v in zip(gmat, vs_b)]
    o0 = [b - a[:, 0:LANES] for a, b in zip(rav, rbv)]
    qmat = [_bf(q - a[:, LANES:2 * LANES]) for a, q in zip(rav, rqs_s)]
    wa = [_dot_tn(w, a) for w, a in zip(wtk, aip_b)]
    nmat = [_dot_tn(v, b) - t[0:LANES] for v, b, t in zip(vs_b, bip_b, wa)]
    mmat = [_bf(t[LANES:2 * LANES]) for t in wa]

    state = [s_ref[p] for p in range(n_pairs)]
    for c in range(nch):
        sb = [_bf(s) for s in state]
        for p in range(n_pairs):
            i = c * n_pairs + p
            o_ref[0, 0, rows[c], lanes[p]] = unstack(_dot_nt(qmat[i], sb[p]) + o0[i])
        state = [state[p] * dec[c * n_pairs + p] - _dot(sb[p], mmat[c * n_pairs + p])
                 + nmat[c * n_pairs + p] for p in range(n_pairs)]
    for p in range(n_pairs):
        s_ref[p] = state[p]


def _rwkv_scan(r, kk, v, lw2, a2, k2, tile):
    b, s, _ = r.shape
    nt = s // tile
    tblk = lambda g, i: i + g * (nt - 1 - 2 * i)
    shared = pl.BlockSpec((1, tile, A_W), lambda bi, g, i: (bi, tblk(g, i), 0))
    perdir = pl.BlockSpec((1, 1, tile, A_W), lambda bi, g, i: (g, bi, tblk(g, i), 0))
    return pl.pallas_call(
        functools.partial(_scan_kernel, nch=tile // SCAN_CHUNK),
        grid=(b, 2, nt),
        in_specs=[shared, shared, shared, perdir, perdir, perdir],
        out_specs=perdir,
        out_shape=jax.ShapeDtypeStruct((2, b, s, A_W), F32),
        scratch_shapes=[pltpu.VMEM((H_A // 2, LANES, LANES), F32)],
        compiler_params=_cparams(("parallel", "arbitrary", "arbitrary")),
        name="rwkv_scan",
    )(r, kk, v, lw2, a2, k2)


def _rope(x, c_t, s1_t, s2_t):
    width = x.shape[-1]
    half = D_ROPE // 2
    return x * c_t + pltpu.roll(x, width - half, 1) * s1_t + pltpu.roll(x, half, 1) * s2_t


def _head_norm_rope(x, w_ref, c_t, s1_t, s2_t):
    outs = []
    for h in range(H_B):
        hs = slice(h * QK_PAD, (h + 1) * QK_PAD)
        xh = x[:, hs]
        ms = jnp.sum(xh * xh, axis=-1, keepdims=True) * (1.0 / D_QK)
        xh = xh * lax.rsqrt(ms + NORM_EPS) * w_ref[:, hs]
        outs.append(_rope(xh, c_t, s1_t, s2_t).astype(BF16))
    return jnp.concatenate(outs, axis=1)


def _mla_prep_kernel(cm_ref, c_ref, s1_ref, s2_ref, qnw_ref, wuq_ref, kvnw_ref, wuk_ref, wuv_ref,
                     place_ref, qkq_ref, qkk_ref, vone_ref, q_ref, k_ref, v_ref):
    cm = cm_ref[0]
    c_t, s1_t, s2_t = c_ref[...], s1_ref[...], s2_ref[...]

    cq = cm[:, 0:2 * LANES]
    msq = jnp.sum(cq * cq, axis=-1, keepdims=True) * (1.0 / Q_RANK)
    cqn = (cq * lax.rsqrt(msq + NORM_EPS) * qnw_ref[...]).astype(BF16)
    q_ref[0] = _head_norm_rope(_dot(cqn, wuq_ref[...]), qkq_ref, c_t, s1_t, s2_t)

    ckv = cm[:, 2 * LANES:3 * LANES]
    mskv = jnp.mean(ckv * ckv, axis=-1, keepdims=True)
    ckvn = (ckv * lax.rsqrt(mskv + NORM_EPS) * kvnw_ref[...]).astype(BF16)
    v_ref[0] = (_dot(ckvn, wuv_ref[...]) + vone_ref[...]).astype(BF16)

    kr = cm[:, 3 * LANES:4 * LANES]
    place = place_ref[...]
    kr_hi = kr.astype(BF16)
    rem = kr - kr_hi.astype(F32)
    kr_mid = rem.astype(BF16)
    kr_lo = (rem - kr_mid.astype(F32)).astype(BF16)
    k = _dot(ckvn, wuk_ref[...]) + _dot(kr_hi, place) + _dot(kr_mid, place) + _dot(kr_lo, place)
    k_ref[0] = _head_norm_rope(k, qkk_ref, c_t, s1_t, s2_t)


def _mla_prep(cm, tabs, mw, ts):
    b, s, _ = cm.shape
    full = lambda shape: pl.BlockSpec(shape, lambda bi, i: (0,) * len(shape))
    tab = pl.BlockSpec((ts, LANES), lambda bi, i: (i, 0))
    out = pl.BlockSpec((1, ts, QKW), lambda bi, i: (bi, i, 0))
    osh = jax.ShapeDtypeStruct((b, s, QKW), BF16)
    return pl.pallas_call(
        _mla_prep_kernel,
        grid=(b, s // ts),
        in_specs=[
            pl.BlockSpec((1, ts, CM_W), lambda bi, i: (bi, i, 0)), tab, tab, tab,
            full((1, 2 * LANES)), full((2 * LANES, QKW)), full((1, LANES)), full((LANES, QKW)),
            full((LANES, QKW)), full((LANES, QKW)), full((1, QKW)), full((1, QKW)), full((1, QKW)),
        ],
        out_specs=[out, out, out],
        out_shape=[osh, osh, osh],
        compiler_params=_cparams(("parallel", "parallel")),
        name="mla_prep",
    )(cm, *tabs, mw["qnw"], mw["wuq"], mw["kvnw"], mw["wuk"], mw["wuv"], mw["place"],
      mw["qkq"], mw["qkk"], mw["vone"])


def _rope_tables(s):
    half = D_ROPE // 2
    freqs = ROPE_THETA ** (-jnp.arange(half, dtype=F32) / half)
    ang = jnp.arange(s, dtype=F32)[:, None] * freqs[None, :]
    cos, sin = jnp.cos(ang), jnp.sin(ang)
    z = lambda n: jnp.zeros((s, n), F32)
    c_t = jnp.concatenate([jnp.ones((s, D_NOPE), F32), cos, cos, z(QK_PAD - D_QK)], axis=1)
    s1_t = jnp.concatenate([z(D_NOPE), -sin, z(half), z(QK_PAD - D_QK)], axis=1)
    s2_t = jnp.concatenate([z(D_NOPE), z(half), sin, z(QK_PAD - D_QK)], axis=1)
    return c_t, s1_t, s2_t


ATTN_ROW_TILE = 32


def _attn_kernel(q_ref, k_ref, v_ref, o_ref, s_scr, p_scr, acc_scr, m_scr, *, tk):
    tq = q_ref.shape[1]
    nk = k_ref.shape[1] // tk
    heads = (slice(0, QK_PAD), slice(QK_PAD, 2 * QK_PAD))

    def kv_rows(j):
        return pl.ds(pl.multiple_of(j * tk, tk), tk)

    def scores(j, slot, h):
        s_scr[slot, h] = _dot_nt(q_ref[0, :, heads[h]], k_ref[0, kv_rows(j), heads[h]])

    def softmax(slot, h):
        for r in range(tq // ATTN_ROW_TILE):
            rs = slice(r * ATTN_ROW_TILE, (r + 1) * ATTN_ROW_TILE)
            s = s_scr[slot, h, rs, :]
            m_old = m_scr[h, rs, :]
            m_new = jnp.maximum(m_old, jnp.max(s, axis=-1, keepdims=True))
            p_scr[h, rs, :] = jnp.exp2(s - jnp.concatenate([m_new] * (tk // LANES), axis=1)).astype(BF16)
            acc_scr[h, rs, :] = acc_scr[h, rs, :] * jnp.exp2(m_old - m_new)
            m_scr[h, rs, :] = m_new

    def accumulate(j, h):
        acc_scr[h] += _dot(p_scr[h], v_ref[0, kv_rows(j), heads[h]])

    def absorb(j, slot):
        for h in range(2):
            softmax(slot, h)
            accumulate(j, h)

    def both_scores(j, slot):
        for h in range(2):
            scores(j, slot, h)

    m_scr[...] = jnp.full(m_scr.shape, -1e30, F32)
    acc_scr[...] = jnp.zeros(acc_scr.shape, F32)
    both_scores(0, 0)

    def body(jj, carry):
        j = 2 * jj
        both_scores(j + 1, 1)
        absorb(j, 0)
        both_scores(j + 2, 0)
        absorb(j + 1, 1)
        return carry

    lax.fori_loop(0, nk // 2 - 1, body, 0)
    both_scores(nk - 1, 1)
    absorb(nk - 2, 0)
    absorb(nk - 1, 1)
    lane = lax.broadcasted_iota(jnp.int32, (tq, QK_PAD), 1)
    acc0, acc1 = acc_scr[0], acc_scr[1]
    den0 = acc0[:, ONES_LANE_EVEN:ONES_LANE_EVEN + 1]
    den1 = acc1[:, ONES_LANE_ODD:ONES_LANE_ODD + 1]
    o_ref[0] = jnp.where(lane < D_V, acc0 / den0, acc1 / den1)


def _attention(q, k, v, tq, tk):
    b, s, _ = q.shape
    assert s % (2 * tk) == 0, "the key loop handles two key blocks per trip"
    pair_w = 2 * QK_PAD
    return pl.pallas_call(
        functools.partial(_attn_kernel, tk=tk),
        grid=(b, H_B // 2, s // tq),
        in_specs=[
            pl.BlockSpec((1, tq, pair_w), lambda bi, hp, i: (bi, i, hp)),
            pl.BlockSpec((1, s, pair_w), lambda bi, hp, i: (bi, 0, hp)),
            pl.BlockSpec((1, s, pair_w), lambda bi, hp, i: (bi, 0, hp)),
        ],
        out_specs=pl.BlockSpec((1, tq, 2 * D_V), lambda bi, hp, i: (bi, i, hp)),
        out_shape=jax.ShapeDtypeStruct((b, s, B_W), F32),
        scratch_shapes=[
            pltpu.VMEM((2, 2, tq, tk), F32),
            pltpu.VMEM((2, tq, tk), BF16),
            pltpu.VMEM((2, tq, QK_PAD), F32),
            pltpu.VMEM((2, tq, LANES), F32),
        ],
        compiler_params=_cparams(("parallel", "parallel", "parallel")),
        name="mla_attention",
    )(q, k, v)


def _sgu_kernel(e_ref, lnw_ref, lnb_ref, wcat_ref, bias_ref, y_ref, *, ts):
    e = e_ref[0]
    u = e[:, 0:C_W]
    v = e[:, C_W:2 * C_W]
    g = e[:, 2 * C_W:3 * C_W]
    mu = jnp.mean(v, axis=-1, keepdims=True)
    d = v - mu
    var = jnp.mean(d * d, axis=-1, keepdims=True)
    vn = d * lax.rsqrt(var + NORM_EPS) * lnw_ref[...] + lnb_ref[...]
    group = lax.broadcasted_iota(jnp.int32, (CHUNK, C_W), 1) // C_GW
    wcat = wcat_ref[...]
    bias = bias_ref[...]
    for c in range(ts // CHUNK):
        rows = slice(c * CHUNK, (c + 1) * CHUNK)
        vc = vn[rows, :]
        stack = jnp.concatenate([jnp.where(group == gi, vc, 0.0) for gi in range(C_GROUPS)], axis=0)
        sv = _dot(wcat, stack.astype(BF16)) + bias
        y_ref[0, rows, :] = u[rows, :] * sv * _silu(g[rows, :])


def _sgu(e, lnw, lnb, wcat, bias, ts):
    b, s, _ = e.shape
    full = lambda shape: pl.BlockSpec(shape, lambda bi, i: (0,) * len(shape))
    return pl.pallas_call(
        functools.partial(_sgu_kernel, ts=ts),
        grid=(b, s // ts),
        in_specs=[pl.BlockSpec((1, ts, E_W), lambda bi, i: (bi, i, 0)),
                  full((1, C_W)), full((1, C_W)), full((CHUNK, C_GROUPS * CHUNK)), full((CHUNK, C_W))],
        out_specs=pl.BlockSpec((1, ts, C_W), lambda bi, i: (bi, i, 0)),
        out_shape=jax.ShapeDtypeStruct((b, s, C_W), F32),
        compiler_params=_cparams(("parallel", "parallel")),
        name="sgu",
    )(e, lnw, lnb, wcat, bias)


def _out_kernel(x_ref, mod_ref, o2_ref, bon_ref, ga_ref, oat_ref, gb_ref, yc_ref,
                gnw_ref, gnb_ref, ones_ref, wa_ref, wb_ref, wc_ref, out_ref):
    ones = ones_ref[...]
    o = o2_ref[0, 0] + o2_ref[1, 0]
    mu = _seg_sum(o, ones) * (1.0 / HEAD_A)
    d = o - mu
    var = _seg_sum(d * d, ones) * (1.0 / HEAD_A)
    on = d * lax.rsqrt(var + GN_EPS)
    ya = (on * gnw_ref[...] + gnb_ref[...] + bon_ref[0]) * _silu(ga_ref[0])
    yb = oat_ref[0] * _silu(gb_ref[0])
    y = (_dot(ya.astype(BF16), wa_ref[...]) + _dot(yb.astype(BF16), wb_ref[...])
         + _dot(yc_ref[0].astype(BF16), wc_ref[...]))
    out_ref[0] = x_ref[0] + mod_ref[0, 2:3, :] * y


def _out_proj(x, mod3, o2, bon, ga, oat, gb, yc, gnw, gnb, ones64, wa, wb, wc, tm):
    b, s, d = x.shape
    tok = lambda width: pl.BlockSpec((1, tm, width), lambda bi, i: (bi, i, 0))
    full = lambda shape: pl.BlockSpec(shape, lambda bi, i: (0,) * len(shape))
    return pl.pallas_call(
        _out_kernel,
        grid=(b, s // tm),
        in_specs=[
            tok(d), pl.BlockSpec((1, 3, d), lambda bi, i: (bi, 0, 0)),
            pl.BlockSpec((2, 1, tm, A_W), lambda bi, i: (0, bi, i, 0)),
            tok(A_W), tok(A_W), tok(B_W), tok(B_W), tok(C_W),
            full((1, A_W)), full((1, A_W)), full((A_W, A_W)),
            full((A_W, d)), full((B_W, d)), full((C_W, d)),
        ],
        out_specs=tok(d),
        out_shape=jax.ShapeDtypeStruct((b, s, d), F32),
        compiler_params=_cparams(("parallel", "parallel")),
        name="out_proj",
    )(x, mod3, o2, bon, ga, oat, gb, yc, gnw, gnb, ones64, wa, wb, wc)


def _pad_cols(w, width):
    return jnp.pad(w, ((0, 0), (0, width - w.shape[1])))


def _per_head_cols(w, real, padded):
    kdim = w.shape[0]
    w = w.reshape(kdim, -1, real)
    return jnp.pad(w, ((0, 0), (0, 0), (0, padded - real))).reshape(kdim, -1)


def _layer_params(l, norm_w, w_in, conv_a, decay_w0, decay_up, iclr_a0, iclr_up, k_k, k_a, r_k,
                  gn_w, gn_b, q_norm_w, w_uq, kv_norm_w, w_ukv, qk_q_norm_w, qk_k_norm_w,
                  sgu_ln_w, sgu_ln_b, w_s, b_s, w_out):
    wi = w_in[l]
    o_ga = A_CONV_COLS
    o_cq = o_ga + A_W
    o_ckv = o_cq + Q_RANK
    o_kr = o_ckv + KV_RANK
    o_gb = o_kr + D_ROPE
    o_e = o_gb + B_W
    w_pad = jnp.concatenate([
        wi[:, :o_cq],
        _pad_cols(wi[:, o_cq:o_ckv], 2 * LANES),
        wi[:, o_ckv:o_kr],
        _pad_cols(wi[:, o_kr:o_gb], LANES),
        wi[:, o_gb:],
    ], axis=1).astype(BF16)

    zeros64 = jnp.zeros((LORA_W, A_W), F32)
    dup2 = jnp.stack([jnp.concatenate([decay_up[l, 0], zeros64], 0),
                      jnp.concatenate([zeros64, decay_up[l, 1]], 0)]).astype(BF16)
    iup2 = jnp.stack([jnp.concatenate([iclr_up[l, 0], zeros64], 0),
                      jnp.concatenate([zeros64, iclr_up[l, 1]], 0)]).astype(BF16)

    scale = D_QK ** -0.5
    wukv = w_ukv[l].reshape(KV_RANK, H_B, D_NOPE + D_V)
    wuk = jnp.pad(wukv[:, :, :D_NOPE], ((0, 0), (0, 0), (0, QK_PAD - D_NOPE))).reshape(KV_RANK, QKW)
    wv = wukv[:, :, D_NOPE:].reshape(KV_RANK, H_B // 2, 2, D_V)
    zv = jnp.zeros((KV_RANK, H_B // 2, D_V), F32)
    wuv = jnp.stack([wv[:, :, 0], zv, zv, wv[:, :, 1]], axis=2).reshape(KV_RANK, QKW)
    lane = jnp.arange(QKW)
    place = ((lane[None, :] % QK_PAD) == (D_NOPE + jnp.arange(LANES)[:, None])) & (jnp.arange(LANES)[:, None] < D_ROPE)
    head_w = lambda w: jnp.tile(jnp.pad(w, (0, QK_PAD - D_QK)), H_B).reshape(1, QKW)
    mw = dict(
        qnw=jnp.pad(q_norm_w[l], (0, 2 * LANES - Q_RANK)).reshape(1, 2 * LANES),
        wuq=jnp.pad(_per_head_cols(w_uq[l], D_QK, QK_PAD), ((0, 2 * LANES - Q_RANK), (0, 0))).astype(BF16),
        kvnw=kv_norm_w[l].reshape(1, KV_RANK),
        wuk=wuk.astype(BF16),
        wuv=wuv.astype(BF16),
        place=place.astype(BF16),
        qkq=head_w(qk_q_norm_w[l]) * (scale * LOG2E),
        qkk=head_w(qk_k_norm_w[l]),
        vone=(((lane % (2 * QK_PAD)) == ONES_LANE_EVEN)
              | ((lane % (2 * QK_PAD)) == QK_PAD + ONES_LANE_ODD)).astype(F32).reshape(1, QKW),
    )
    wcat = jnp.transpose(w_s[l], (1, 0, 2)).reshape(CHUNK, C_GROUPS * CHUNK).astype(BF16)
    bias = jnp.repeat(b_s[l].T, C_GW, axis=1)
    wo = w_out[l].astype(BF16)
    return dict(
        norm_w=norm_w[l], w_pad=w_pad, conv_a=conv_a[l], decay_w0=decay_w0[l], dup2=dup2,
        iclr_a0=iclr_a0[l], iup2=iup2, k_k=k_k[l], k_a=k_a[l], r_k=r_k[l].reshape(A_W),
        gnw=gn_w[l].reshape(1, A_W), gnb=gn_b[l].reshape(1, A_W), ones64=_block_ones(A_W, HEAD_A),
        mw=mw, lnw=sgu_ln_w[l].reshape(1, C_W), lnb=sgu_ln_b[l].reshape(1, C_W), wcat=wcat, bias=bias,
        wa=wo[:A_W], wb=wo[A_W:A_W + B_W], wc=wo[A_W + B_W:],
    )


TOKEN_TILE = 256
ATTN_TQ = 256
ATTN_TK = 512


def _layer(x, mod3, p, tabs):
    b, s, _ = x.shape
    za, ga, cm, gb, e = _in_proj(x, mod3, p["norm_w"], p["w_pad"], TOKEN_TILE)

    r, kk, v, bon, lw2, akk2, krep2 = _rwkv_prep(
        za, p["conv_a"], p["decay_w0"], p["dup2"], p["iclr_a0"], p["iup2"],
        p["k_k"], p["k_a"], p["r_k"], p["ones64"], TOKEN_TILE)
    o2 = _rwkv_scan(r, kk, v, lw2, akk2, krep2, TOKEN_TILE)

    q, k, vv = _mla_prep(cm, tabs, p["mw"], TOKEN_TILE)
    oat = _attention(q, k, vv, ATTN_TQ, min(ATTN_TK, s // 2))

    yc = _sgu(e, p["lnw"], p["lnb"], p["wcat"], p["bias"], TOKEN_TILE)

    return _out_proj(x, mod3, o2, bon, ga, oat, gb, yc, p["gnw"], p["gnb"], p["ones64"],
                     p["wa"], p["wb"], p["wc"], TOKEN_TILE)


def kernel(x_prompt, x_sample, c_prompt, c_sample, norm_w, ada_w, ada_b, w_in, conv_a, decay_w0, decay_up, iclr_a0, iclr_up, k_k, k_a, r_k, gn_w, gn_b, q_norm_w, w_uq, kv_norm_w, w_ukv, qk_q_norm_w, qk_k_norm_w, sgu_ln_w, sgu_ln_b, w_s, b_s, w_out):
    depth = norm_w.shape[0]
    bp = x_prompt.shape[0]
    mod = _ada_mod(jnp.concatenate([c_prompt, c_sample], axis=0), ada_w, ada_b)
    mod = mod.reshape(depth, -1, 3, D_MODEL)
    tabs_p = _rope_tables(x_prompt.shape[1])
    tabs_s = _rope_tables(x_sample.shape[1])
    y_p, y_s = x_prompt, x_sample
    for l in range(depth):
        p = _layer_params(l, norm_w, w_in, conv_a, decay_w0, decay_up, iclr_a0, iclr_up, k_k, k_a, r_k,
                          gn_w, gn_b, q_norm_w, w_uq, kv_norm_w, w_ukv, qk_q_norm_w, qk_k_norm_w,
                          sgu_ln_w, sgu_ln_b, w_s, b_s, w_out)
        y_p = _layer(y_p, mod[l, :bp], p, tabs_p)
        y_s = _layer(y_s, mod[l, bp:], p, tabs_s)
    return (y_p, y_s)
```
